```python
import math
import jax, jax.numpy as jnp
from jax import lax
import numpy as np

D_MODEL = 2048
BATCH = 2
SEQ = 4096
DEPTH = 4
DEC_BATCH = 8
DEC_SEQ = 8
PAST_LEN = 16384
PAGE_SIZE = 128

HEAD_DIM = 128
HALF_DIM = HEAD_DIM // 2
N_HEADS_A = 8
N_HEADS_B = 8
N_HEADS_ATT = N_HEADS_A + N_HEADS_B
D_ATT = N_HEADS_ATT * HEAD_DIM
MOBA_BLOCK = 256
MOBA_TOPK = 3
Q_BLOCK = 128
MOBA_Q_BLOCK = 64
REL_BUCKETS = 32
REL_MAX_DIST = 128
CONV_WIDTH = 3
N_GROUPS = 4
EXPERTS_PER_GROUP = 8
N_EXPERTS = N_GROUPS * EXPERTS_PER_GROUP
TOP_EXPERT = 2
D_EXPERT = 512
MOE_BLOCK = 128
N_ATT_LAYERS = (DEPTH + 1) // 2
N_CONV_LAYERS = DEPTH // 2
DEEPNORM_ALPHA = (2.0 * DEPTH) ** 0.25
DEEPNORM_BETA = (8.0 * DEPTH) ** -0.25
LN_EPS = 1e-5
NEG_INF = -1e30

kernel_name = 'hybrid_diffattn_moba_shortconv_hmoe_step'


def t5_bucket(n):
    n = jnp.maximum(n, 0)
    max_exact = REL_BUCKETS // 2
    nf = jnp.maximum(n, 1).astype(jnp.float32)
    large = max_exact + (jnp.log(nf / max_exact) / math.log(REL_MAX_DIST / max_exact)
                         * (REL_BUCKETS - max_exact)).astype(jnp.int32)
    return jnp.where(n < max_exact, n, jnp.minimum(large, REL_BUCKETS - 1))


def rel_bias(qpos, kpos, table_t):
    return table_t[:, t5_bucket(qpos[:, None] - kpos[None, :])]


def layer_norm(x, g, b):
    xf = x.astype(jnp.float32)
    mu = jnp.mean(xf, -1, keepdims=True)
    var = jnp.mean(jnp.square(xf - mu), -1, keepdims=True)
    return ((xf - mu) * lax.rsqrt(var + LN_EPS)).astype(x.dtype) * g + b


def rms_norm(x, g):
    xf = x.astype(jnp.float32)
    return (xf * lax.rsqrt(jnp.mean(xf * xf, -1, keepdims=True) + LN_EPS)).astype(x.dtype) * g


def modulate(x, shift, scale):
    return x * (1 + scale[:, None, :]) + shift[:, None, :]


def split_qkv(h, w_qkv):
    b, t, _ = h.shape
    qkv = (h @ w_qkv).reshape(b, t, 3, N_HEADS_ATT, HEAD_DIM)
    return qkv[:, :, 0], qkv[:, :, 1], qkv[:, :, 2]


def diff_attn_core(q1, q2, k1, k2, v, mask, bias, lam):
    scale = HALF_DIM ** -0.5
    s1 = jnp.einsum('bqhd,bkhd->bhqk', q1, k1).astype(jnp.float32) * scale + bias
    s2 = jnp.einsum('bqhd,bkhd->bhqk', q2, k2).astype(jnp.float32) * scale + bias
    p = (jax.nn.softmax(jnp.where(mask, s1, NEG_INF), axis=-1)
         - lam * jax.nn.softmax(jnp.where(mask, s2, NEG_INF), axis=-1))
    return jnp.einsum('bhqk,bkhd->bqhd', p.astype(v.dtype), v)


def diff_attn_prompt(q, k, v, table_t, lam):
    b, s, h, d = q.shape
    k1, k2 = k[..., :HALF_DIM], k[..., HALF_DIM:]
    kpos = jnp.arange(s)

    def one(i):
        q0 = i * Q_BLOCK
        qb = lax.dynamic_slice_in_dim(q, q0, Q_BLOCK, axis=1)
        qpos = q0 + jnp.arange(Q_BLOCK)
        mask = kpos[None, :] <= qpos[:, None]
        return diff_attn_core(qb[..., :HALF_DIM], qb[..., HALF_DIM:], k1, k2, v, mask,
                              rel_bias(qpos, kpos, table_t), lam)

    o = lax.map(one, jnp.arange(s // Q_BLOCK))
    return jnp.moveaxis(o, 0, 1).reshape(b, s, h, d)


def moba_core(q, qpos, k_own, v_own, kpos_own, k_blocks, v_blocks, sel, sel_ok, table_t):
    scale = HEAD_DIM ** -0.5
    s_own = jnp.einsum('bqhd,bkhd->bhqk', q, k_own).astype(jnp.float32) * scale + rel_bias(qpos, kpos_own, table_t)
    s_own = jnp.where(kpos_own[None, :] <= qpos[:, None], s_own, NEG_INF)
    if sel is None:
        p = jax.nn.softmax(s_own, axis=-1).astype(v_own.dtype)
        return jnp.einsum('bhqk,bkhd->bqhd', p, v_own)
    b, nq, h, d = q.shape
    n = sel.shape[-1]
    bi = jnp.arange(b)[:, None, None, None]
    hi = jnp.arange(h)[None, :, None, None]
    k_sel = k_blocks[bi, sel, :, hi]
    v_sel = v_blocks[bi, sel, :, hi]
    kpos_sel = sel[..., None] * MOBA_BLOCK + jnp.arange(MOBA_BLOCK)
    bucket = t5_bucket(qpos[None, None, :, None, None] - kpos_sel)
    bias_sel = table_t[hi[..., None], bucket]
    s_sel = jnp.einsum('bqhd,bhqnkd->bhqnk', q, k_sel).astype(jnp.float32) * scale + bias_sel
    if sel_ok is not None:
        s_sel = jnp.where(sel_ok[..., None], s_sel, NEG_INF)
    s_sel = s_sel.reshape(b, h, nq, n * MOBA_BLOCK)
    p = jax.nn.softmax(jnp.concatenate([s_sel, s_own], axis=-1), axis=-1).astype(v_own.dtype)
    o_sel = jnp.einsum('bhqk,bhqkd->bqhd', p[..., :n * MOBA_BLOCK], v_sel.reshape(b, h, nq, n * MOBA_BLOCK, d))
    return o_sel + jnp.einsum('bhqk,bkhd->bqhd', p[..., n * MOBA_BLOCK:], v_own)


def moba_prompt(q, k, v, table_t):
    b, s, h, d = q.shape
    nb = -(-s // MOBA_BLOCK)
    pad = nb * MOBA_BLOCK - s
    kp = jnp.pad(k, ((0, 0), (0, pad), (0, 0), (0, 0)))
    vp = jnp.pad(v, ((0, 0), (0, pad), (0, 0), (0, 0)))
    k_blocks = kp.reshape(b, nb, MOBA_BLOCK, h, d)
    v_blocks = vp.reshape(b, nb, MOBA_BLOCK, h, d)
    n_sel = min(MOBA_TOPK, nb - 1)
    own = jnp.arange(s) // MOBA_BLOCK
    if n_sel > 0:
        gate = jnp.einsum('bshd,bnhd->bhsn', q, jnp.mean(k_blocks, axis=2)).astype(jnp.float32)
        gate = jnp.where(jnp.arange(nb)[None, :] < own[:, None], gate, NEG_INF)
        sel = lax.top_k(gate, n_sel)[1]
        sel_ok = sel < own[:, None]

    def one(i):
        q0 = i * MOBA_Q_BLOCK
        blk0 = (q0 // MOBA_BLOCK) * MOBA_BLOCK
        qc = lax.dynamic_slice_in_dim(q, q0, MOBA_Q_BLOCK, axis=1)
        qpos = q0 + jnp.arange(MOBA_Q_BLOCK)
        k_own = lax.dynamic_slice_in_dim(kp, blk0, MOBA_BLOCK, axis=1)
        v_own = lax.dynamic_slice_in_dim(vp, blk0, MOBA_BLOCK, axis=1)
        kpos_own = blk0 + jnp.arange(MOBA_BLOCK)
        if n_sel > 0:
            sel_c = lax.dynamic_slice_in_dim(sel, q0, MOBA_Q_BLOCK, axis=2)
            ok_c = lax.dynamic_slice_in_dim(sel_ok, q0, MOBA_Q_BLOCK, axis=2)
            return moba_core(qc, qpos, k_own, v_own, kpos_own, k_blocks, v_blocks, sel_c, ok_c, table_t)
        return moba_core(qc, qpos, k_own, v_own, kpos_own, None, None, None, None, table_t)

    o = lax.map(one, jnp.arange(s // MOBA_Q_BLOCK))
    return jnp.moveaxis(o, 0, 1).reshape(b, s, h, d)


def moba_sample(q, k_new, v_new, k_past, v_past, table_t):
    b, t, h, d = q.shape
    past = k_past.shape[1]
    n_full = past // MOBA_BLOCK
    own0 = n_full * MOBA_BLOCK
    qpos = past + jnp.arange(t)
    k_own = jnp.concatenate([k_past[:, own0:], k_new], axis=1)
    v_own = jnp.concatenate([v_past[:, own0:], v_new], axis=1)
    kpos_own = own0 + jnp.arange(k_own.shape[1])
    n_sel = min(MOBA_TOPK, n_full)
    if n_sel == 0:
        return moba_core(q, qpos, k_own, v_own, kpos_own, None, None, None, None, table_t)
    k_blocks = k_past[:, :own0].reshape(b, n_full, MOBA_BLOCK, h, d)
    v_blocks = v_past[:, :own0].reshape(b, n_full, MOBA_BLOCK, h, d)
    gate = jnp.einsum('bthd,bnhd->bhtn', q, jnp.mean(k_blocks, axis=2)).astype(jnp.float32)
    sel = lax.top_k(gate, n_sel)[1]
    return moba_core(q, qpos, k_own, v_own, kpos_own, k_blocks, v_blocks, sel, None, table_t)


def merge_heads(o_a, o_b, subln_g, lam_init, w_o):
    b, t = o_a.shape[:2]
    o_a = rms_norm(o_a, subln_g) * (1.0 - lam_init)
    return jnp.concatenate([o_a, o_b], axis=2).reshape(b, t, D_ATT) @ w_o


def attention_prompt(h, w_qkv, w_o, tab_a, tab_b, lam, lam_init, subln_g):
    q, k, v = split_qkv(h, w_qkv)
    na = N_HEADS_A
    o_a = diff_attn_prompt(q[:, :, :na], k[:, :, :na], v[:, :, :na], tab_a, lam)
    o_b = moba_prompt(q[:, :, na:], k[:, :, na:], v[:, :, na:], tab_b)
    return merge_heads(o_a, o_b, subln_g, lam_init, w_o), k, v


def attention_sample(h, k_past, v_past, w_qkv, w_o, tab_a, tab_b, lam, lam_init, subln_g):
    b, t, _ = h.shape
    past = k_past.shape[1]
    na = N_HEADS_A
    q, k, v = split_qkv(h, w_qkv)
    qpos = past + jnp.arange(t)
    kpos = jnp.arange(past + t)
    ka = jnp.concatenate([k_past[:, :, :na], k[:, :, :na]], axis=1)
    va = jnp.concatenate([v_past[:, :, :na], v[:, :, :na]], axis=1)
    mask = kpos[None, :] <= qpos[:, None]
    o_a = diff_attn_core(q[:, :, :na, :HALF_DIM], q[:, :, :na, HALF_DIM:], ka[..., :HALF_DIM], ka[..., HALF_DIM:],
                         va, mask, rel_bias(qpos, kpos, tab_a), lam)
    o_b = moba_sample(q[:, :, na:], k[:, :, na:], v[:, :, na:], k_past[:, :, na:], v_past[:, :, na:], tab_b)
    return merge_heads(o_a, o_b, subln_g, lam_init, w_o), k, v


def conv_mixer(h, conv_state, w_in, conv_w, w_out):
    t = h.shape[1]
    gate_b, gate_c, hv = jnp.split(h @ w_in, 3, axis=-1)
    u_ext = jnp.concatenate([conv_state, gate_c * hv], axis=1)
    conv = sum(conv_w[j] * u_ext[:, j:j + t] for j in range(CONV_WIDTH))
    return (gate_b * conv) @ w_out, u_ext[:, -(CONV_WIDTH - 1):]


def routed_experts(xf, experts, weights, w1, w3, w2):
    n_tok, d = xf.shape
    nk = experts.size
    flat_e = experts.reshape(nk)
    order = jnp.argsort(flat_e)
    e_sorted = flat_e[order]
    tok_sorted = order // experts.shape[1]
    w_sorted = weights.reshape(nk)[order]
    counts = jnp.bincount(flat_e, length=N_EXPERTS)
    padded = (counts + MOE_BLOCK - 1) // MOE_BLOCK * MOE_BLOCK
    pad_end = jnp.cumsum(padded)
    pad_start = pad_end - padded
    start = jnp.cumsum(counts) - counts
    dest = pad_start[e_sorted] + jnp.arange(nk) - start[e_sorted]
    n_blocks = (nk + MOE_BLOCK - 1) // MOE_BLOCK + N_EXPERTS
    slot_tok = jnp.full((n_blocks * MOE_BLOCK,), n_tok, jnp.int32).at[dest].set(tok_sorted)
    block_expert = jnp.minimum(jnp.searchsorted(pad_end, jnp.arange(n_blocks) * MOE_BLOCK, side='right'),
                               N_EXPERTS - 1)
    x_slots = jnp.concatenate([xf, jnp.zeros((1, d), xf.dtype)], axis=0)[slot_tok].reshape(n_blocks, MOE_BLOCK, d)

    def expert_block(args):
        xb, e = args
        return (jax.nn.silu(xb @ w1[e]) * (xb @ w3[e])) @ w2[e]

    y_slots = lax.map(expert_block, (x_slots, block_expert)).reshape(n_blocks * MOE_BLOCK, d)
    return jnp.zeros_like(xf).at[tok_sorted].add(y_slots[dest] * w_sorted[:, None])


def hier_moe(x, w_group, b_group, w_expert, b_expert, w1, w3, w2):
    b, t, d = x.shape
    xf = x.reshape(b * t, d)
    g_prob = jax.nn.softmax((xf @ w_group + b_group).astype(jnp.float32), axis=-1)
    g_p, g_idx = lax.top_k(g_prob, 1)
    e_logits = (xf @ w_expert + b_expert).astype(jnp.float32).reshape(b * t, N_GROUPS, EXPERTS_PER_GROUP)
    e_logits = jnp.take_along_axis(e_logits, g_idx[:, :, None], axis=1)[:, 0]
    e_p, e_idx = lax.top_k(jax.nn.softmax(e_logits, axis=-1), TOP_EXPERT)
    e_p = e_p / jnp.sum(e_p, axis=-1, keepdims=True)
    weights = (g_p * e_p).astype(x.dtype)
    experts = g_idx * EXPERTS_PER_GROUP + e_idx
    return routed_experts(xf, experts, weights, w1, w3, w2).reshape(b, t, d)


def adaln(c, w, b):
    return (jax.nn.silu(c) @ w + b).reshape(c.shape[0], 6, c.shape[1])


def setup_inputs(seed: int = 0) -> dict:
    key = jax.random.key(seed)
    keys = iter(jax.random.split(key, 32))

    def nrm(shape, s):
        return jax.random.normal(next(keys), shape, jnp.float32) * s

    n_pages = PAST_LEN // PAGE_SIZE
    n_pool = (DEC_BATCH * n_pages * 5) // 4
    d = D_MODEL
    page_table = jax.random.permutation(next(keys), n_pool)[:DEC_BATCH * n_pages]
    page_table = page_table.reshape(DEC_BATCH, n_pages).astype(jnp.int32)
    return {
        'x_prompt': nrm((BATCH, SEQ, d), 1.0),
        'x_sample': nrm((DEC_BATCH, DEC_SEQ, d), 1.0),
        'cache_k': nrm((N_ATT_LAYERS, n_pool, PAGE_SIZE, N_HEADS_ATT, HEAD_DIM), 1.0),
        'cache_v': nrm((N_ATT_LAYERS, n_pool, PAGE_SIZE, N_HEADS_ATT, HEAD_DIM), 1.0),
        'state_conv': nrm((N_CONV_LAYERS, DEC_BATCH, CONV_WIDTH - 1, d), 1.0),
        'page_table': page_table,
        'c_prompt': nrm((BATCH, d), 1.0),
        'c_sample': nrm((DEC_BATCH, d), 1.0),
        'rel_bias_table': nrm((REL_BUCKETS, N_HEADS_ATT), 0.5),
        'att_w_qkv': nrm((N_ATT_LAYERS, d, 3 * D_ATT), d ** -0.5),
        'att_w_o': nrm((N_ATT_LAYERS, D_ATT, d), D_ATT ** -0.5 * DEEPNORM_BETA),
        'lambda_q1': nrm((N_ATT_LAYERS, HALF_DIM), 0.1),
        'lambda_k1': nrm((N_ATT_LAYERS, HALF_DIM), 0.1),
        'lambda_q2': nrm((N_ATT_LAYERS, HALF_DIM), 0.1),
        'lambda_k2': nrm((N_ATT_LAYERS, HALF_DIM), 0.1),
        'subln_g': 1.0 + nrm((N_ATT_LAYERS, HEAD_DIM), 0.02),
        'conv_w_in': nrm((N_CONV_LAYERS, d, 3 * d), d ** -0.5),
        'conv_w': nrm((N_CONV_LAYERS, CONV_WIDTH, d), CONV_WIDTH ** -0.5),
        'conv_w_out': nrm((N_CONV_LAYERS, d, d), d ** -0.5 * DEEPNORM_BETA),
        'ada_w': nrm((DEPTH, d, 6 * d), 0.1 * d ** -0.5),
        'ada_b': nrm((DEPTH, 6 * d), 0.02),
        'ln_g': 1.0 + nrm((DEPTH, 2, d), 0.02),
        'ln_b': nrm((DEPTH, 2, d), 0.02),
        'moe_w_group': nrm((DEPTH, d, N_GROUPS), d ** -0.5),
        'moe_b_group': nrm((DEPTH, N_GROUPS), 0.01),
        'moe_w_expert': nrm((DEPTH, d, N_EXPERTS), d ** -0.5),
        'moe_b_expert': nrm((DEPTH, N_EXPERTS), 0.01),
        'moe_w1': nrm((DEPTH, N_EXPERTS, d, D_EXPERT), d ** -0.5),
        'moe_w3': nrm((DEPTH, N_EXPERTS, d, D_EXPERT), d ** -0.5),
        'moe_w2': nrm((DEPTH, N_EXPERTS, D_EXPERT, d), D_EXPERT ** -0.5 * DEEPNORM_BETA),
    }


def reference(x_prompt, x_sample, cache_k, cache_v, state_conv, page_table, c_prompt, c_sample,
              rel_bias_table, att_w_qkv, att_w_o, lambda_q1, lambda_k1, lambda_q2, lambda_k2, subln_g,
              conv_w_in, conv_w, conv_w_out, ada_w, ada_b, ln_g, ln_b,
              moe_w_group, moe_b_group, moe_w_expert, moe_b_expert, moe_w1, moe_w3, moe_w2):
    tab_a = rel_bias_table[:, :N_HEADS_A].T
    tab_b = rel_bias_table[:, N_HEADS_A:].T
    db, n_pages = page_table.shape
    past_len = n_pages * PAGE_SIZE
    xp, xs = x_prompt, x_sample
    kp_rows, vp_rows, ks_rows, vs_rows, conv_p, conv_s = [], [], [], [], [], []
    for l in range(DEPTH):
        mod_p = adaln(c_prompt, ada_w[l], ada_b[l])
        mod_s = adaln(c_sample, ada_w[l], ada_b[l])
        hp = modulate(xp, mod_p[:, 0], mod_p[:, 1])
        hs = modulate(xs, mod_s[:, 0], mod_s[:, 1])
        if l % 2 == 0:
            a = l // 2
            lam_init = 0.8 - 0.6 * math.exp(-0.3 * l)
            lam = (jnp.exp(jnp.sum((lambda_q1[a] * lambda_k1[a]).astype(jnp.float32)))
                   - jnp.exp(jnp.sum((lambda_q2[a] * lambda_k2[a]).astype(jnp.float32))) + lam_init)
            fp, k_new, v_new = attention_prompt(hp, att_w_qkv[a], att_w_o[a], tab_a, tab_b, lam, lam_init, subln_g[a])
            k_past = cache_k[a, page_table].reshape(db, past_len, N_HEADS_ATT, HEAD_DIM)
            v_past = cache_v[a, page_table].reshape(db, past_len, N_HEADS_ATT, HEAD_DIM)
            fs, ks_new, vs_new = attention_sample(hs, k_past, v_past, att_w_qkv[a], att_w_o[a], tab_a, tab_b,
                                                  lam, lam_init, subln_g[a])
            kp_rows.append(k_new)
            vp_rows.append(v_new)
            ks_rows.append(ks_new)
            vs_rows.append(vs_new)
        else:
            ci = l // 2
            zero_state = jnp.zeros((hp.shape[0], CONV_WIDTH - 1, hp.shape[2]), hp.dtype)
            fp, st_p = conv_mixer(hp, zero_state, conv_w_in[ci], conv_w[ci], conv_w_out[ci])
            fs, st_s = conv_mixer(hs, state_conv[ci], conv_w_in[ci], conv_w[ci], conv_w_out[ci])
            conv_p.append(st_p)
            conv_s.append(st_s)
        xp = layer_norm(DEEPNORM_ALPHA * xp + (1 + mod_p[:, 2, None, :]) * fp, ln_g[l, 0], ln_b[l, 0])
        xs = layer_norm(DEEPNORM_ALPHA * xs + (1 + mod_s[:, 2, None, :]) * fs, ln_g[l, 0], ln_b[l, 0])
        hp = modulate(xp, mod_p[:, 3], mod_p[:, 4])
        hs = modulate(xs, mod_s[:, 3], mod_s[:, 4])
        gp = hier_moe(hp, moe_w_group[l], moe_b_group[l], moe_w_expert[l], moe_b_expert[l], moe_w1[l], moe_w3[l], moe_w2[l])
        gs = hier_moe(hs, moe_w_group[l], moe_b_group[l], moe_w_expert[l], moe_b_expert[l], moe_w1[l], moe_w3[l], moe_w2[l])
        xp = layer_norm(DEEPNORM_ALPHA * xp + (1 + mod_p[:, 5, None, :]) * gp, ln_g[l, 1], ln_b[l, 1])
        xs = layer_norm(DEEPNORM_ALPHA * xs + (1 + mod_s[:, 5, None, :]) * gs, ln_g[l, 1], ln_b[l, 1])
    k_prompt = jnp.stack(kp_rows)
    v_prompt = jnp.stack(vp_rows)
    conv_prompt = jnp.stack(conv_p)
    k_sample = jnp.stack(ks_rows)
    v_sample = jnp.stack(vs_rows)
    conv_sample = jnp.stack(conv_s)
    return (xp, xs, k_prompt, v_prompt, conv_prompt, k_sample, v_sample, conv_sample)
```

```python
import functools
import math

import jax
import jax.numpy as jnp
from jax import lax
from jax.experimental import pallas as pl
from jax.experimental.pallas import tpu as pltpu

F32, BF16, I32 = jnp.float32, jnp.bfloat16, jnp.int32
HIGHEST = lax.Precision.HIGHEST

HEAD_DIM = 128
HALF_DIM = HEAD_DIM // 2
N_HEADS_A = 8
N_HEADS_B = 8
N_HEADS = N_HEADS_A + N_HEADS_B
MOBA_BLOCK = 256
MOBA_TOPK = 3
REL_BUCKETS = 32
REL_MAX_DIST = 128
N_GROUPS = 4
EXPERTS_PER_GROUP = 8
N_EXPERTS = N_GROUPS * EXPERTS_PER_GROUP
CONV_WIDTH = 3
LN_EPS = 1e-5
NEG_INF = -1e30

LANES = 128
ATT_TILE = MOBA_BLOCK
ROW_TILE = 256
MOE_ROWS = 256
PAGES_PER_STEP = 4
VMEM_LIMIT = 56 << 20


def _cparams(n_axes, vmem=VMEM_LIMIT):
    return pltpu.CompilerParams(dimension_semantics=("arbitrary",) * n_axes, vmem_limit_bytes=vmem)


def _smem_spec():
    return pl.BlockSpec(memory_space=pltpu.SMEM)


def _layer_norm(y, g, b):
    mu = jnp.mean(y, axis=-1, keepdims=True)
    yc = y - mu
    var = jnp.mean(yc * yc, axis=-1, keepdims=True)
    return yc * lax.rsqrt(var + LN_EPS) * g + b


def _silu(x):
    return x * (1.0 / (1.0 + jnp.exp(-x)))


def _dot_nt(a, b):
    return lax.dot_general(a, b, (((1,), (1,)), ((), ())), preferred_element_type=F32)


def _adaln_kernel(c_ref, w_ref, b_ref, o_ref):
    s = _silu(c_ref[...])
    o_ref[...] = jnp.dot(s, w_ref[...], precision=HIGHEST, preferred_element_type=F32) + b_ref[...]


def _adaln(c_all, ada_w, ada_b):
    depth, d, n = ada_w.shape
    rows = c_all.shape[0]
    tn = 1024
    return pl.pallas_call(
        _adaln_kernel,
        grid=(depth, n // tn),
        in_specs=[pl.BlockSpec((rows, d), lambda l, j: (0, 0)),
                  pl.BlockSpec((None, d, tn), lambda l, j: (l, 0, j)),
                  pl.BlockSpec((None, 1, tn), lambda l, j: (l, 0, j))],
        out_specs=pl.BlockSpec((None, rows, tn), lambda l, j: (l, 0, j)),
        out_shape=jax.ShapeDtypeStruct((depth, rows, n), F32),
        compiler_params=_cparams(2),
        name="adaln",
    )(c_all, ada_w, ada_b.reshape(depth, 1, n))


class _Mod:
    def __init__(self, table, rows_per_seq, row_tile, per_row):
        self.table, self.rows_per_seq, self.row_tile, self.per_row = table, rows_per_seq, row_tile, per_row

    def spec(self, layer, which, tn=None, row_arg=0, col_arg=None):
        d = self.table.shape[-1]
        tn = d if tn is None else tn
        tiles_per_seq = max(self.rows_per_seq // self.row_tile, 1)
        n_seq = self.table.shape[1] if not self.per_row else None

        def index(*g):
            col = 0 if col_arg is None else g[col_arg]
            if self.per_row:
                return (layer * 6 + which, g[row_arg], col)
            return ((layer * n_seq + g[row_arg] // tiles_per_seq) * 6 + which, 0, col)

        rows = self.row_tile if self.per_row else 1
        return pl.BlockSpec((None, rows, tn), index)

    @property
    def array(self):
        if self.per_row:
            return self.table
        depth, n_seq, six, d = self.table.shape
        return self.table.reshape(depth * n_seq * six, 1, d)


def _vec_spec(d):
    return pl.BlockSpec((1, d), lambda *g: (0, 0))


def _modulate_kernel(x_ref, shift_ref, scale_ref, h_ref):
    h_ref[...] = (x_ref[...] * (1.0 + scale_ref[...]) + shift_ref[...]).astype(h_ref.dtype)


def _modulate(x, mod, layer, tm):
    m, d = x.shape
    return pl.pallas_call(
        _modulate_kernel,
        grid=(m // tm,),
        in_specs=[pl.BlockSpec((tm, d), lambda i: (i, 0)), mod.spec(layer, 0), mod.spec(layer, 1)],
        out_specs=pl.BlockSpec((tm, d), lambda i: (i, 0)),
        out_shape=jax.ShapeDtypeStruct((m, d), BF16),
        compiler_params=_cparams(1),
        name="modulate",
    )(x, mod.array, mod.array)


def _matmul_kernel(x_ref, w_ref, *refs, n_alias):
    acc = jnp.dot(x_ref[...], w_ref[...], preferred_element_type=F32)
    for o_ref in refs[n_alias:]:
        o_ref[...] = acc.astype(o_ref.dtype)


def _project(x, w, *, tm, tn, out_dtypes, stacked=None, layer=0, n_stack=1):
    m, k = x.shape
    n = w.shape[1]
    in_specs = [pl.BlockSpec((tm, k), lambda i, j: (i, 0)), pl.BlockSpec((k, tn), lambda i, j: (0, j))]
    args = [x, w]
    out_specs, out_shape, aliases = [], [], {}
    n_alias = 0
    for dt in out_dtypes:
        if dt == F32 and stacked is not None:
            out_specs.append(pl.BlockSpec((None, tm, tn), lambda i, j: (layer, i, j)))
            out_shape.append(jax.ShapeDtypeStruct((n_stack, m, n), F32))
            if stacked is not True:
                in_specs.append(pl.BlockSpec(memory_space=pl.ANY))
                args.append(stacked)
                aliases[len(args) - 1] = len(out_shape) - 1
                n_alias = 1
        else:
            out_specs.append(pl.BlockSpec((tm, tn), lambda i, j: (i, j)))
            out_shape.append(jax.ShapeDtypeStruct((m, n), dt))
    return pl.pallas_call(
        functools.partial(_matmul_kernel, n_alias=n_alias),
        grid=(m // tm, n // tn),
        in_specs=in_specs, out_specs=out_specs, out_shape=out_shape,
        input_output_aliases=aliases,
        compiler_params=_cparams(2),
        name="project",
    )(*args)


def _t5_bucket(n):
    n = jnp.maximum(n, 0)
    max_exact = REL_BUCKETS // 2
    nf = jnp.maximum(n, 1).astype(F32)
    large = max_exact + (jnp.log(nf / max_exact) / math.log(REL_MAX_DIST / max_exact)
                         * (REL_BUCKETS - max_exact)).astype(I32)
    return jnp.where(n < max_exact, n, jnp.minimum(large, REL_BUCKETS - 1))


def _bias_of_distance(table, dist):
    return jnp.moveaxis(table[_t5_bucket(dist)], -1, 0)


def _prompt_bias_tiles(table, t):
    assert t >= REL_MAX_DIST
    kk = jnp.arange(t)[:, None]
    qq = jnp.arange(t)[None, :]
    d0 = qq - kk
    same = jnp.where(d0 >= 0, _bias_of_distance(table, d0), NEG_INF)
    prev = _bias_of_distance(table, d0 + t)
    far = jnp.broadcast_to(table[REL_BUCKETS - 1][:, None, None], prev.shape)
    return jnp.stack([same, prev, far], axis=1).astype(F32)


def _transpose_v(v_ref, vt_ref, t):
    for c in range(v_ref.shape[0] // t):
        vt_ref[c] = v_ref[c * t:(c + 1) * t, :].astype(F32).T.astype(BF16)


def _softmax_block(s, vt, m_ref, l_ref, acc_ref, idx):
    m_old = m_ref[idx]
    m_new = jnp.maximum(m_old, jnp.max(s, axis=0, keepdims=True))
    a = jnp.exp(m_old - m_new)
    p = jnp.exp(s - m_new)
    l_ref[idx] = a * l_ref[idx] + jnp.sum(p, axis=0, keepdims=True)
    acc_ref[idx] = a * acc_ref[idx] + jnp.dot(vt, p.astype(BF16), preferred_element_type=F32)
    m_ref[idx] = m_new


def _init_stats(m_ref, l_ref, acc_ref):
    m_ref[...] = jnp.full(m_ref.shape, NEG_INF, F32)
    l_ref[...] = jnp.zeros(l_ref.shape, F32)
    acc_ref[...] = jnp.zeros(acc_ref.shape, F32)


def _diff_attn_kernel(lam_ref, q_ref, k_ref, v_ref, bias_ref, g_ref, o_ref,
                      vt_ref, m_ref, l_ref, acc_ref, *, t, out_scale):
    qi = pl.program_id(2)

    @pl.when(qi == 0)
    def _():
        _transpose_v(v_ref, vt_ref, t)

    q = q_ref[...].astype(F32) * (HALF_DIM ** -0.5)
    lane = lax.broadcasted_iota(I32, q.shape, 1)
    qs = (jnp.where(lane < HALF_DIM, q, 0.0).astype(BF16), jnp.where(lane >= HALF_DIM, q, 0.0).astype(BF16))
    _init_stats(m_ref, l_ref, acc_ref)

    def block(j, bias):
        kb = k_ref[pl.ds(pl.multiple_of(j * t, t), t), :]
        vt = vt_ref[j]
        for idx in range(2):
            _softmax_block(_dot_nt(kb, qs[idx]) + bias, vt, m_ref, l_ref, acc_ref, idx)

    block(qi, bias_ref[0])

    def body(j, c):
        block(j, bias_ref[jnp.minimum(qi - j, 2)])
        return c

    lax.fori_loop(0, qi, body, 0)

    o = acc_ref[0] / l_ref[0] - lam_ref[0] * (acc_ref[1] / l_ref[1])
    ms = jnp.mean(o * o, axis=0, keepdims=True)
    o = o * lax.rsqrt(ms + LN_EPS) * g_ref[...] * out_scale
    o_ref[...] = o.T.astype(o_ref.dtype)


def _moba_attn_kernel(q_ref, k_ref, v_ref, bias_ref, o_in_ref, o_ref,
                      vt_ref, kmean_ref, sel_ref, m_ref, l_ref, acc_ref, *, t):
    del o_in_ref
    qi = pl.program_id(2)
    nb = k_ref.shape[0] // t

    @pl.when(qi == 0)
    def _():
        _transpose_v(v_ref, vt_ref, t)
        for c in range(nb):
            kmean_ref[c:c + 1, :] = jnp.mean(k_ref[c * t:(c + 1) * t, :].astype(F32), axis=0, keepdims=True)

    q = q_ref[...].astype(F32)
    gate = lax.dot_general(kmean_ref[...], q, (((1,), (1,)), ((), ())), precision=HIGHEST,
                           preferred_element_type=F32)
    blk = lax.broadcasted_iota(I32, gate.shape, 0)
    rank = jnp.zeros(gate.shape, F32)
    for n in range(nb):
        row = gate[n:n + 1, :]
        beats = jnp.logical_or(row > gate, jnp.logical_and(row == gate, n < blk))
        rank = rank + jnp.where(jnp.logical_and(beats, n < qi), 1.0, 0.0)
    sel_ref[...] = jnp.where(jnp.logical_and(blk < qi, rank < MOBA_TOPK), 1.0, 0.0)

    qs = (q * (HEAD_DIM ** -0.5)).astype(BF16)
    _init_stats(m_ref, l_ref, acc_ref)
    kb = k_ref[pl.ds(pl.multiple_of(qi * t, t), t), :]
    _softmax_block(_dot_nt(kb, qs) + bias_ref[0], vt_ref[qi], m_ref, l_ref, acc_ref, 0)

    def body(j, c):
        kb = k_ref[pl.ds(pl.multiple_of(j * t, t), t), :]
        s = _dot_nt(kb, qs) + bias_ref[jnp.minimum(qi - j, 2)]
        s = jnp.where(sel_ref[pl.ds(j, 1), :] > 0.0, s, NEG_INF)
        _softmax_block(s, vt_ref[j], m_ref, l_ref, acc_ref, 0)
        return c

    lax.fori_loop(0, qi, body, 0)
    o_ref[...] = (acc_ref[0] / l_ref[0]).T.astype(o_ref.dtype)


def _prompt_attention(q, k, v, tiles, lam, subln_g, lam_init, n_batch, seq):
    m, d_att = q.shape
    t = ATT_TILE
    assert seq % t == 0 and t == MOBA_BLOCK
    nq = seq // t
    nb = seq // t
    na = N_HEADS_A

    def specs(h0):
        return [pl.BlockSpec((t, HEAD_DIM), lambda b, h, i: (b * nq + i, h0 + h)),
                pl.BlockSpec((seq, HEAD_DIM), lambda b, h, i: (b, h0 + h)),
                pl.BlockSpec((seq, HEAD_DIM), lambda b, h, i: (b, h0 + h)),
                pl.BlockSpec((None, 3, t, t), lambda b, h, i: (h0 + h, 0, 0, 0))]

    stats = [pltpu.VMEM((2, 1, t), F32), pltpu.VMEM((2, 1, t), F32), pltpu.VMEM((2, HEAD_DIM, t), F32)]
    o = pl.pallas_call(
        functools.partial(_diff_attn_kernel, t=t, out_scale=1.0 - lam_init),
        grid=(n_batch, na, nq),
        in_specs=[_smem_spec()] + specs(0) + [pl.BlockSpec((HEAD_DIM, 1), lambda b, h, i: (0, 0))],
        out_specs=pl.BlockSpec((t, HEAD_DIM), lambda b, h, i: (b * nq + i, h)),
        out_shape=jax.ShapeDtypeStruct((m, d_att), BF16),
        scratch_shapes=[pltpu.VMEM((nb, HEAD_DIM, t), BF16)] + stats,
        compiler_params=_cparams(3),
        name="diff_attention",
    )(lam.reshape(1), q, k, v, tiles, subln_g.reshape(HEAD_DIM, 1))
    return pl.pallas_call(
        functools.partial(_moba_attn_kernel, t=t),
        grid=(n_batch, N_HEADS_B, nq),
        in_specs=specs(na) + [pl.BlockSpec(memory_space=pl.ANY)],
        out_specs=pl.BlockSpec((t, HEAD_DIM), lambda b, h, i: (b * nq + i, na + h)),
        out_shape=jax.ShapeDtypeStruct((m, d_att), BF16),
        scratch_shapes=[pltpu.VMEM((nb, HEAD_DIM, t), BF16), pltpu.VMEM((nb, HEAD_DIM), F32),
                        pltpu.VMEM((nb, t), F32)] + stats,
        input_output_aliases={4: 0},
        compiler_params=_cparams(3),
        name="moba_attention",
    )(q, k, v, tiles, o)


def _decode_kernel(pt_ref, lam_ref, far_ref, q_ref, knew_ref, vnew_ref, dlast_ref, bnew_ref, g_ref, *refs,
                   n_pages_step, n_steps, out_scale):
    del pt_ref
    p_step = n_pages_step
    k_pages = refs[:p_step]
    v_pages = refs[p_step:2 * p_step]
    o_ref = refs[2 * p_step]
    (kpad_ref, vpad_ref, m_ref, l_ref, acc_ref, gate_ref, pm_ref, pl_ref, po_ref) = refs[2 * p_step + 1:]
    step = pl.program_id(1)
    t_new = q_ref.shape[0]
    pages_per_blk = MOBA_BLOCK // LANES
    blks_per_step = p_step // pages_per_blk
    last_f = jnp.where(step == n_steps - 1, 1.0, 0.0)
    na = N_HEADS_A

    @pl.when(step == 0)
    def _():
        _init_stats(m_ref, l_ref, acc_ref)

    def head(ref, h):
        return ref[:, h * HEAD_DIM:(h + 1) * HEAD_DIM]

    def diff_queries(h):
        qh = head(q_ref, h) * (HALF_DIM ** -0.5)
        lane = lax.broadcasted_iota(I32, qh.shape, 1)
        return jnp.concatenate([jnp.where(lane < HALF_DIM, qh, 0.0), jnp.where(lane >= HALF_DIM, qh, 0.0)],
                               axis=0).astype(BF16)

    def diff_update(h, s, v_list):
        m_old = m_ref[h]
        m_new = jnp.maximum(m_old, jnp.max(s, axis=1, keepdims=True))
        a = jnp.exp(m_old - m_new)
        p = jnp.exp(s - m_new)
        l_ref[h] = a * l_ref[h] + jnp.sum(p, axis=1, keepdims=True)
        pb = p.astype(BF16)
        pv = sum(jnp.dot(pb[:, u * LANES:(u + 1) * LANES], v_list[u], preferred_element_type=F32)
                 for u in range(len(v_list)))
        acc_ref[h] = a * acc_ref[h] + pv
        m_ref[h] = m_new

    def block_partial(s, v_list):
        m = jnp.max(s, axis=1, keepdims=True)
        p = jnp.exp(s - m)
        pb = p.astype(BF16)
        o = sum(jnp.dot(pb[:, u * LANES:(u + 1) * LANES], v_list[u], preferred_element_type=F32)
                for u in range(len(v_list)))
        return m, jnp.sum(p, axis=1, keepdims=True), o

    for h in range(na):
        q12 = diff_queries(h)
        parts = [_dot_nt(q12, head(k_pages[u], h).astype(BF16)) for u in range(p_step)]
        parts[-1] = parts[-1] + dlast_ref[h] * last_f
        s = jnp.concatenate(parts, axis=1) + far_ref[h]
        diff_update(h, s, [head(v_pages[u], h).astype(BF16) for u in range(p_step)])

    for hb in range(N_HEADS_B):
        h = na + hb
        qh = (head(q_ref, h) * (HEAD_DIM ** -0.5)).astype(BF16)
        for c in range(blks_per_step):
            us = range(c * pages_per_blk, (c + 1) * pages_per_blk)
            parts = [_dot_nt(qh, head(k_pages[u], h).astype(BF16)) for u in us]
            raw = jnp.concatenate(parts, axis=1)
            if c == blks_per_step - 1:
                parts[-1] = parts[-1] + dlast_ref[h][:t_new] * last_f
            s = jnp.concatenate(parts, axis=1) + far_ref[h]
            m, l, o = block_partial(s, [head(v_pages[u], h).astype(BF16) for u in us])
            n = step * blks_per_step + c
            gate_ref[hb, n] = jnp.sum(raw, axis=1, keepdims=True)
            pm_ref[hb, n] = m
            pl_ref[hb, n] = l
            po_ref[hb, n] = o

    @pl.when(step == n_steps - 1)
    def _():
        kpad_ref[...] = jnp.zeros(kpad_ref.shape, F32)
        vpad_ref[...] = jnp.zeros(vpad_ref.shape, F32)
        kpad_ref[0:t_new, :] = knew_ref[...]
        vpad_ref[0:t_new, :] = vnew_ref[...]
        lam = lam_ref[0]
        for h in range(na):
            s = _dot_nt(diff_queries(h), head(kpad_ref, h).astype(BF16)) + bnew_ref[h]
            diff_update(h, s, [head(vpad_ref, h).astype(BF16)])
            o = acc_ref[h] / l_ref[h]
            o = o[:t_new] - lam * o[t_new:]
            ms = jnp.mean(o * o, axis=1, keepdims=True)
            o_ref[:, h * HEAD_DIM:(h + 1) * HEAD_DIM] = o * lax.rsqrt(ms + LN_EPS) * g_ref[...] * out_scale
        for hb in range(N_HEADS_B):
            h = na + hb
            qh = (head(q_ref, h) * (HEAD_DIM ** -0.5)).astype(BF16)
            s = _dot_nt(qh, head(kpad_ref, h).astype(BF16)) + bnew_ref[h][:t_new]
            m_own, l_own, o_own = block_partial(s, [head(vpad_ref, h).astype(BF16)])
            gate = gate_ref[hb]
            idx = lax.broadcasted_iota(I32, gate.shape, 0).astype(F32)
            sel = jnp.zeros(gate.shape, jnp.bool_)
            for _ in range(min(MOBA_TOPK, gate.shape[0])):
                best = jnp.max(gate, axis=0, keepdims=True)
                first = jnp.min(jnp.where(gate == best, idx, float(gate.shape[0])), axis=0, keepdims=True)
                pick = idx == first
                sel = jnp.logical_or(sel, pick)
                gate = jnp.where(pick, -jnp.inf, gate)
            pm = pm_ref[hb]
            m_all = jnp.maximum(m_own, jnp.max(jnp.where(sel, pm, NEG_INF), axis=0))
            w = jnp.where(sel, jnp.exp(pm - m_all), 0.0)
            w_own = jnp.exp(m_own - m_all)
            l_all = w_own * l_own + jnp.sum(w * pl_ref[hb], axis=0)
            o_all = w_own * o_own + jnp.sum(w * po_ref[hb], axis=0)
            o_ref[:, h * HEAD_DIM:(h + 1) * HEAD_DIM] = o_all / l_all


def _sample_attention(q, k_new, v_new, cache_k, cache_v, page_table, layer, table, lam, subln_g, lam_init):
    n_seq, n_pages = page_table.shape
    t_new = q.shape[0] // n_seq
    d_att = q.shape[1]
    page = cache_k.shape[2]
    past = n_pages * page
    p_step = PAGES_PER_STEP
    assert page == LANES and n_pages % p_step == 0 and past % MOBA_BLOCK == 0 and p_step % (MOBA_BLOCK // page) == 0
    assert page >= REL_MAX_DIST and t_new == 8
    n_steps = n_pages // p_step
    n_blocks = past // MOBA_BLOCK
    ck = cache_k.reshape(cache_k.shape[0], cache_k.shape[1], page, d_att)
    cv = cache_v.reshape(cache_v.shape[0], cache_v.shape[1], page, d_att)

    tt = jnp.arange(t_new)[:, None]
    rr = jnp.arange(page)[None, :]
    far = table[REL_BUCKETS - 1]
    dlast = _bias_of_distance(table, page + tt - rr) - far[:, None, None]
    dnew = tt - rr
    bnew = jnp.where(jnp.logical_and(dnew >= 0, rr < t_new), _bias_of_distance(table, dnew), NEG_INF)
    dlast = jnp.concatenate([dlast, dlast], axis=1).astype(F32)
    bnew = jnp.concatenate([bnew, bnew], axis=1).astype(F32)

    def page_spec(u):
        return pl.BlockSpec((None, None, page, d_att),
                            lambda b, s, pt: (layer, pt[b * n_pages + s * p_step + u], 0, 0))

    row_spec = pl.BlockSpec((t_new, d_att), lambda b, s, pt: (b, 0))
    full3 = pl.BlockSpec((N_HEADS, 2 * t_new, page), lambda b, s, pt: (0, 0, 0))
    grid_spec = pltpu.PrefetchScalarGridSpec(
        num_scalar_prefetch=1,
        grid=(n_seq, n_steps),
        in_specs=[_smem_spec(), _smem_spec(), row_spec, row_spec, row_spec, full3, full3,
                  pl.BlockSpec((1, HEAD_DIM), lambda b, s, pt: (0, 0))]
                 + [page_spec(u) for u in range(p_step)] * 2,
        out_specs=row_spec,
        scratch_shapes=[pltpu.VMEM((page, d_att), F32), pltpu.VMEM((page, d_att), F32),
                        pltpu.VMEM((N_HEADS_A, 2 * t_new, 1), F32), pltpu.VMEM((N_HEADS_A, 2 * t_new, 1), F32),
                        pltpu.VMEM((N_HEADS_A, 2 * t_new, HEAD_DIM), F32),
                        pltpu.VMEM((N_HEADS_B, n_blocks, t_new, 1), F32),
                        pltpu.VMEM((N_HEADS_B, n_blocks, t_new, 1), F32),
                        pltpu.VMEM((N_HEADS_B, n_blocks, t_new, 1), F32),
                        pltpu.VMEM((N_HEADS_B, n_blocks, t_new, HEAD_DIM), F32)],
    )
    return pl.pallas_call(
        functools.partial(_decode_kernel, n_pages_step=p_step, n_steps=n_steps, out_scale=1.0 - lam_init),
        grid_spec=grid_spec,
        out_shape=jax.ShapeDtypeStruct(q.shape, F32),
        compiler_params=_cparams(2),
        name="decode_attention",
    )(page_table.reshape(-1), lam.reshape(1), far.astype(F32), q, k_new, v_new, dlast, bnew,
      subln_g.reshape(1, HEAD_DIM), *([ck] * p_step), *([cv] * p_step))


def _conv_in_kernel(x_ref, wb_ref, wc_ref, wv_ref, cw_ref, st_ref, z_ref, tail_ref, carry_ref, *, tiles_per_seq):
    i = pl.program_id(1)
    x = x_ref[...]
    gb = jnp.dot(x, wb_ref[...], preferred_element_type=F32)
    u = jnp.dot(x, wc_ref[...], preferred_element_type=F32) * jnp.dot(x, wv_ref[...], preferred_element_type=F32)
    tm = u.shape[0]

    @pl.when(i % tiles_per_seq == 0)
    def _():
        carry_ref[...] = st_ref[...]

    prev = carry_ref[...]
    row = lax.broadcasted_iota(I32, u.shape, 0)
    u1 = jnp.where(row == 0, prev[1:2], pltpu.roll(u, 1, axis=0))
    u2 = jnp.where(row == 0, prev[0:1], jnp.where(row == 1, prev[1:2], pltpu.roll(u, 2, axis=0)))
    cw = cw_ref[...]
    z_ref[...] = (gb * (cw[0:1] * u2 + cw[1:2] * u1 + cw[2:3] * u)).astype(z_ref.dtype)
    tail = u[tm - (CONV_WIDTH - 1):tm]
    carry_ref[...] = tail
    tail_ref[...] = tail


def _conv_in(h, w_in, conv_w, state, *, rows_per_seq, tm, tn):
    m, d = h.shape
    n_seq = m // rows_per_seq
    assert rows_per_seq % tm == 0 and tm >= CONV_WIDTH - 1
    tiles_per_seq = rows_per_seq // tm
    nj = d // tn
    st_spec = pl.BlockSpec((None, CONV_WIDTH - 1, tn), lambda j, i: (i // tiles_per_seq, 0, j))
    return pl.pallas_call(
        functools.partial(_conv_in_kernel, tiles_per_seq=tiles_per_seq),
        grid=(nj, m // tm),
        in_specs=[pl.BlockSpec((tm, d), lambda j, i: (i, 0))] + _w_in_specs(d, tn, nj, lambda j, i: j)
                 + [pl.BlockSpec((CONV_WIDTH, tn), lambda j, i: (0, j)), st_spec],
        out_specs=[pl.BlockSpec((tm, tn), lambda j, i: (i, j)), st_spec],
        out_shape=[jax.ShapeDtypeStruct((m, d), BF16), jax.ShapeDtypeStruct((n_seq, CONV_WIDTH - 1, d), F32)],
        scratch_shapes=[pltpu.VMEM((CONV_WIDTH - 1, tn), F32)],
        compiler_params=_cparams(2),
        name="conv_in",
    )(h, w_in, w_in, w_in, conv_w, state)


def _w_in_specs(d, tn, nj, col_of):
    def spec(part):
        return pl.BlockSpec((d, tn), lambda *g: (0, part * nj + col_of(*g)))
    return [spec(part) for part in range(3)]


def _conv_in_short_kernel(x_ref, wb_ref, wc_ref, wv_ref, cw_ref, p0_ref, p1_ref, z_ref, u_ref, *, t):
    x = x_ref[...]
    gb = jnp.dot(x, wb_ref[...], preferred_element_type=F32)
    u = jnp.dot(x, wc_ref[...], preferred_element_type=F32) * jnp.dot(x, wv_ref[...], preferred_element_type=F32)
    pos = lax.rem(lax.broadcasted_iota(I32, u.shape, 0), t)
    u1 = jnp.where(pos == 0, p1_ref[...], pltpu.roll(u, 1, axis=0))
    u2 = jnp.where(pos == 0, p0_ref[...], jnp.where(pos == 1, p1_ref[...], pltpu.roll(u, 2, axis=0)))
    cw = cw_ref[...]
    z_ref[...] = (gb * (cw[0:1] * u2 + cw[1:2] * u1 + cw[2:3] * u)).astype(z_ref.dtype)
    u_ref[...] = u


def _conv_in_short(h, w_in, conv_w, state, *, t, tn):
    m, d = h.shape
    n_seq = m // t
    assert t >= CONV_WIDTH - 1
    nj = d // tn
    p0 = jnp.repeat(state[:, 0, :], t, axis=0)
    p1 = jnp.repeat(state[:, 1, :], t, axis=0)
    col = pl.BlockSpec((m, tn), lambda j: (0, j))
    z, u = pl.pallas_call(
        functools.partial(_conv_in_short_kernel, t=t),
        grid=(nj,),
        in_specs=[pl.BlockSpec((m, d), lambda j: (0, 0))] + _w_in_specs(d, tn, nj, lambda j: j)
                 + [pl.BlockSpec((CONV_WIDTH, tn), lambda j: (0, j)), col, col],
        out_specs=[col, col],
        out_shape=[jax.ShapeDtypeStruct((m, d), BF16), jax.ShapeDtypeStruct((m, d), F32)],
        compiler_params=_cparams(1),
        name="conv_in_short",
    )(h, w_in, w_in, w_in, conv_w, p0, p1)
    return z, u.reshape(n_seq, t, d)[:, t - (CONV_WIDTH - 1):]


def _route(logits):
    lane = lax.broadcasted_iota(I32, logits.shape, 1).astype(F32)
    g_mask = lane < N_GROUPS
    g_exp = jnp.exp(jnp.where(g_mask, logits, -jnp.inf)
                    - jnp.max(jnp.where(g_mask, logits, -jnp.inf), axis=1, keepdims=True))
    g_prob = g_exp / jnp.sum(g_exp, axis=1, keepdims=True)
    g_p = jnp.max(g_prob, axis=1, keepdims=True)
    g_idx = jnp.min(jnp.where(jnp.logical_and(g_prob == g_p, g_mask), lane, float(LANES)), axis=1, keepdims=True)
    lo = N_GROUPS + EXPERTS_PER_GROUP * g_idx
    e_mask = jnp.logical_and(lane >= lo, lane < lo + EXPERTS_PER_GROUP)
    e_logit = jnp.where(e_mask, logits, -jnp.inf)
    e_exp = jnp.exp(e_logit - jnp.max(e_logit, axis=1, keepdims=True))
    e_prob = jnp.where(e_mask, e_exp / jnp.sum(e_exp, axis=1, keepdims=True), -1.0)
    p1 = jnp.max(e_prob, axis=1, keepdims=True)
    i1 = jnp.min(jnp.where(e_prob == p1, lane, float(LANES)), axis=1, keepdims=True)
    rest = jnp.where(lane == i1, -1.0, e_prob)
    p2 = jnp.max(rest, axis=1, keepdims=True)
    i2 = jnp.min(jnp.where(rest == p2, lane, float(LANES)), axis=1, keepdims=True)
    tot = p1 + p2
    ids = jnp.where(lane == 0.0, i1 - N_GROUPS, jnp.where(lane == 1.0, i2 - N_GROUPS, 0.0)).astype(I32)
    wts = jnp.where(lane == 0.0, g_p * (p1 / tot), jnp.where(lane == 1.0, g_p * (p2 / tot), 0.0))
    return ids, wts


def _post_mixer_kernel(o_ref, w_ref, x_ref, gate_ref, lng_ref, lnb_ref, shift_ref, scale_ref, wr_ref, br_ref,
                       *refs, alpha, n_alias):
    x1_ref, h2_ref, ids_ref, wts_ref = refs[n_alias:]
    f = jnp.dot(o_ref[...].astype(BF16), w_ref[...], preferred_element_type=F32)
    x1 = _layer_norm(alpha * x_ref[...] + (1.0 + gate_ref[...]) * f, lng_ref[...], lnb_ref[...])
    x1_ref[...] = x1
    h2 = x1 * (1.0 + scale_ref[...]) + shift_ref[...]
    h2_ref[...] = h2
    logits = jnp.dot(h2, wr_ref[...], precision=HIGHEST, preferred_element_type=F32) + br_ref[...]
    ids_ref[...], wts_ref[...] = _route(logits)


def _post_mixer(o, w, x, mod, layer, ln_g, ln_b, w_router, b_router, *, tm, alpha, h2_all, row0, n_all):
    m, d = x.shape
    k = o.shape[1]
    assert row0 % tm == 0
    blk0 = row0 // tm
    row = lambda i: (i, 0)
    in_specs = [pl.BlockSpec((tm, k), row), pl.BlockSpec((k, d), lambda i: (0, 0)), pl.BlockSpec((tm, d), row),
                mod.spec(layer, 2), _vec_spec(d), _vec_spec(d), mod.spec(layer, 3), mod.spec(layer, 4),
                pl.BlockSpec((d, LANES), lambda i: (0, 0)), _vec_spec(LANES)]
    args = [o, w, x, mod.array, ln_g.reshape(1, d), ln_b.reshape(1, d), mod.array, mod.array, w_router, b_router]
    aliases, n_alias = {}, 0
    if h2_all is not None:
        in_specs.append(pl.BlockSpec(memory_space=pl.ANY))
        args.append(h2_all)
        aliases, n_alias = {len(args) - 1: 1}, 1
    return pl.pallas_call(
        functools.partial(_post_mixer_kernel, alpha=alpha, n_alias=n_alias),
        grid=(m // tm,),
        in_specs=in_specs,
        out_specs=[pl.BlockSpec((tm, d), row), pl.BlockSpec((tm, d), lambda i: (blk0 + i, 0)),
                   pl.BlockSpec((tm, LANES), row), pl.BlockSpec((tm, LANES), row)],
        out_shape=[jax.ShapeDtypeStruct((m, d), F32), jax.ShapeDtypeStruct((n_all, d), F32),
                   jax.ShapeDtypeStruct((m, LANES), I32), jax.ShapeDtypeStruct((m, LANES), F32)],
        input_output_aliases=aliases,
        compiler_params=_cparams(1),
        name="post_mixer",
    )(*args)


def _row_copy(src_hbm, row, dst_vmem, slot, sem):
    return pltpu.make_async_copy(src_hbm.at[pl.ds(row, 1), :], dst_vmem.at[pl.ds(slot, 1), :], sem)


def _moe_kernel(be_ref, tok_ref, nused_ref, h_hbm, w1_ref, w3_ref, w2_ref, sw_ref, y_ref,
                xbuf, w1b, w3b, w2b, sem, *, rows):
    i = pl.program_id(0)
    used = i < nused_ref[0]

    @pl.when(used)
    def _():
        base = i * rows

        def issue(r, c):
            _row_copy(h_hbm, tok_ref[base + r], xbuf, r, sem).start()
            return c

        lax.fori_loop(0, rows, issue, 0)

        @pl.when(jnp.logical_or(i == 0, be_ref[i] != be_ref[jnp.maximum(i - 1, 0)]))
        def _():
            w1b[...] = w1_ref[...].astype(BF16)
            w3b[...] = w3_ref[...].astype(BF16)
            w2b[...] = w2_ref[...].astype(BF16)

        def wait(r, c):
            _row_copy(h_hbm, 0, xbuf, r, sem).wait()
            return c

        lax.fori_loop(0, rows, wait, 0)
        x = xbuf[...].astype(BF16)
        a = jnp.dot(x, w1b[...], preferred_element_type=F32)
        b = jnp.dot(x, w3b[...], preferred_element_type=F32)
        y = jnp.dot((_silu(a) * b).astype(BF16), w2b[...], preferred_element_type=F32)
        y_ref[...] = y * sw_ref[...]

    @pl.when(jnp.logical_not(used))
    def _():
        y_ref[...] = jnp.zeros(y_ref.shape, F32)


def _moe_dispatch(ids, wts, rows):
    n_tok, k = ids.shape
    nk = n_tok * k
    flat_e = ids.reshape(nk)
    onehot = (flat_e[:, None] == jnp.arange(N_EXPERTS, dtype=I32)[None, :]).astype(I32)
    csum = jnp.cumsum(onehot, axis=0)
    counts = csum[-1]
    pos = jnp.sum((csum - onehot) * onehot, axis=1)
    padded = (counts + rows - 1) // rows * rows
    pad_end = jnp.cumsum(padded)
    pad_start = pad_end - padded
    dest = (pad_start[flat_e] + pos).astype(I32)
    n_blocks = (nk + rows - 1) // rows + N_EXPERTS
    tok = jnp.arange(nk, dtype=I32) // k
    slot_tok = jnp.zeros((n_blocks * rows,), I32).at[dest].set(tok)
    slot_w = jnp.zeros((n_blocks * rows,), F32).at[dest].set(wts.reshape(nk))
    block_expert = jnp.minimum(jnp.searchsorted(pad_end, jnp.arange(n_blocks, dtype=I32) * rows, side='right'),
                               N_EXPERTS - 1).astype(I32)
    n_used = (pad_end[-1] // rows).astype(I32).reshape(1)
    return dest, slot_tok, slot_w.reshape(-1, 1), block_expert, n_used


def _moe_experts(h_all, slot_tok, slot_w, block_expert, n_used, w1, w3, w2, layer):
    n_slots = slot_tok.shape[0]
    rows = MOE_ROWS
    d = h_all.shape[1]
    d_e = w1.shape[-1]
    grid_spec = pltpu.PrefetchScalarGridSpec(
        num_scalar_prefetch=3,
        grid=(n_slots // rows,),
        in_specs=[pl.BlockSpec(memory_space=pl.ANY),
                  pl.BlockSpec((None, None, d, d_e), lambda i, be, tok, nu: (layer, be[i], 0, 0)),
                  pl.BlockSpec((None, None, d, d_e), lambda i, be, tok, nu: (layer, be[i], 0, 0)),
                  pl.BlockSpec((None, None, d_e, d), lambda i, be, tok, nu: (layer, be[i], 0, 0)),
                  pl.BlockSpec((rows, 1), lambda i, be, tok, nu: (i, 0))],
        out_specs=pl.BlockSpec((rows, d), lambda i, be, tok, nu: (i, 0)),
        scratch_shapes=[pltpu.VMEM((rows, d), F32), pltpu.VMEM((d, d_e), BF16), pltpu.VMEM((d, d_e), BF16),
                        pltpu.VMEM((d_e, d), BF16), pltpu.SemaphoreType.DMA(())],
    )
    return pl.pallas_call(
        functools.partial(_moe_kernel, rows=rows),
        grid_spec=grid_spec,
        out_shape=jax.ShapeDtypeStruct((n_slots, d), F32),
        compiler_params=_cparams(1),
        name="moe_experts",
    )(block_expert, slot_tok, n_used, h_all, w1, w3, w2, slot_w)


def _combine_kernel(dest_ref, x_ref, y_hbm, gate_ref, lng_ref, lnb_ref, *refs, tm, alpha, has_next):
    if has_next:
        shift_ref, scale_ref, x2_ref, h_ref, ybuf, sem = refs
    else:
        x2_ref, ybuf, sem = refs
    base = pl.program_id(0) * (2 * tm)

    def issue(r, c):
        for k in range(2):
            _row_copy(y_hbm, dest_ref[base + 2 * r + k], ybuf.at[k], r, sem).start()
        return c

    lax.fori_loop(0, tm, issue, 0)

    def wait(r, c):
        for k in range(2):
            _row_copy(y_hbm, 0, ybuf.at[k], r, sem).wait()
        return c

    lax.fori_loop(0, tm, wait, 0)
    g = ybuf[0] + ybuf[1]
    x2 = _layer_norm(alpha * x_ref[...] + (1.0 + gate_ref[...]) * g, lng_ref[...], lnb_ref[...])
    x2_ref[...] = x2
    if has_next:
        h_ref[...] = (x2 * (1.0 + scale_ref[...]) + shift_ref[...]).astype(h_ref.dtype)


def _combine(x1, y_slots, dest, mod, layer, ln_g, ln_b, *, tm, alpha, has_next):
    m, d = x1.shape
    row = lambda i, dst: (i, 0)
    vec = pl.BlockSpec((1, d), lambda i, dst: (0, 0))
    in_specs = [pl.BlockSpec((tm, d), row), pl.BlockSpec(memory_space=pl.ANY), mod.spec(layer, 5), vec, vec]
    args = [x1, y_slots, mod.array, ln_g.reshape(1, d), ln_b.reshape(1, d)]
    out_specs = [pl.BlockSpec((tm, d), row)]
    out_shape = [jax.ShapeDtypeStruct((m, d), F32)]
    if has_next:
        in_specs += [mod.spec(layer + 1, 0), mod.spec(layer + 1, 1)]
        args += [mod.array, mod.array]
        out_specs.append(pl.BlockSpec((tm, d), row))
        out_shape.append(jax.ShapeDtypeStruct((m, d), BF16))
    grid_spec = pltpu.PrefetchScalarGridSpec(
        num_scalar_prefetch=1, grid=(m // tm,), in_specs=in_specs, out_specs=out_specs,
        scratch_shapes=[pltpu.VMEM((2, tm, d), F32), pltpu.SemaphoreType.DMA(())])
    out = pl.pallas_call(
        functools.partial(_combine_kernel, tm=tm, alpha=alpha, has_next=has_next),
        grid_spec=grid_spec, out_shape=out_shape, compiler_params=_cparams(1), name="moe_combine",
    )(dest, *args)
    return (out[0], out[1]) if has_next else (out[0], None)


def kernel(x_prompt, x_sample, cache_k, cache_v, state_conv, page_table, c_prompt, c_sample, rel_bias_table,
           att_w_qkv, att_w_o, lambda_q1, lambda_k1, lambda_q2, lambda_k2, subln_g, conv_w_in, conv_w, conv_w_out,
           ada_w, ada_b, ln_g, ln_b, moe_w_group, moe_b_group, moe_w_expert, moe_b_expert, moe_w1, moe_w3, moe_w2):
    n_batch, seq, d = x_prompt.shape
    n_dec, t_new, _ = x_sample.shape
    depth = ada_w.shape[0]
    n_att = att_w_qkv.shape[0]
    d_att = N_HEADS * HEAD_DIM
    alpha = (2.0 * depth) ** 0.25
    mp, ms = n_batch * seq, n_dec * t_new
    n_all = mp + ms
    tm_p, tm_s = ROW_TILE, ms
    assert mp % ms == 0 and seq % tm_p == 0 and cache_k.shape[3] == N_HEADS and cache_k.shape[4] == HEAD_DIM

    n_seq = n_batch + n_dec
    pad = -n_seq % 8
    c_all = jnp.concatenate([c_prompt, c_sample, jnp.zeros((pad, d), F32)], axis=0)
    mod_all = _adaln(c_all, ada_w, ada_b).reshape(depth, n_seq + pad, 6, d)
    mod_p = _Mod(mod_all[:, :n_batch], seq, tm_p, per_row=False)
    mod_s_rows = jnp.repeat(jnp.swapaxes(mod_all[:, n_batch:n_seq], 1, 2), t_new, axis=2)
    mod_s = _Mod(mod_s_rows.reshape(depth * 6, ms, d), t_new, tm_s, per_row=True)

    xp = x_prompt.reshape(mp, d)
    xs = x_sample.reshape(ms, d)
    hp = _modulate(xp, mod_p, 0, tm_p)
    hs = _modulate(xs, mod_s, 0, tm_s)

    tiles = _prompt_bias_tiles(rel_bias_table, ATT_TILE)
    zero_state = jnp.zeros((n_batch, CONV_WIDTH - 1, d), F32)
    kp_buf = vp_buf = True
    ks_rows, vs_rows, conv_p, conv_s = [], [], [], []

    for l in range(depth):
        if l % 2 == 0:
            a = l // 2
            lam_init = 0.8 - 0.6 * math.exp(-0.3 * l)
            lam = (jnp.exp(jnp.sum(lambda_q1[a] * lambda_k1[a])) - jnp.exp(jnp.sum(lambda_q2[a] * lambda_k2[a]))
                   + lam_init).astype(F32)
            w_qkv = att_w_qkv[a].astype(BF16)
            wq, wk, wv = w_qkv[:, :d_att], w_qkv[:, d_att:2 * d_att], w_qkv[:, 2 * d_att:]
            (q_p,) = _project(hp, wq, tm=1024, tn=1024, out_dtypes=[BF16])
            kp_buf, k_p = _project(hp, wk, tm=1024, tn=1024, out_dtypes=[F32, BF16], stacked=kp_buf, layer=a,
                                   n_stack=n_att)
            vp_buf, v_p = _project(hp, wv, tm=1024, tn=1024, out_dtypes=[F32, BF16], stacked=vp_buf, layer=a,
                                   n_stack=n_att)
            o_p = _prompt_attention(q_p, k_p, v_p, tiles, lam, subln_g[a], lam_init, n_batch, seq)
            (qkv_s,) = _project(hs, w_qkv, tm=ms, tn=1024, out_dtypes=[F32])
            q_s, k_s, v_s = qkv_s[:, :d_att], qkv_s[:, d_att:2 * d_att], qkv_s[:, 2 * d_att:]
            ks_rows.append(k_s.reshape(n_dec, t_new, N_HEADS, HEAD_DIM))
            vs_rows.append(v_s.reshape(n_dec, t_new, N_HEADS, HEAD_DIM))
            o_s = _sample_attention(q_s, k_s, v_s, cache_k, cache_v, page_table, a, rel_bias_table, lam,
                                    subln_g[a], lam_init)
            w_out = att_w_o[a].astype(BF16)
        else:
            ci = l // 2
            w_in = conv_w_in[ci].astype(BF16)
            o_p, st_p = _conv_in(hp, w_in, conv_w[ci], zero_state, rows_per_seq=seq, tm=512, tn=512)
            o_s, st_s = _conv_in_short(hs, w_in, conv_w[ci], state_conv[ci], t=t_new, tn=512)
            conv_p.append(st_p)
            conv_s.append(st_s)
            w_out = conv_w_out[ci].astype(BF16)

        w_router = jnp.zeros((d, LANES), F32).at[:, :N_GROUPS].set(moe_w_group[l]) \
            .at[:, N_GROUPS:N_GROUPS + N_EXPERTS].set(moe_w_expert[l])
        b_router = jnp.zeros((1, LANES), F32).at[0, :N_GROUPS].set(moe_b_group[l]) \
            .at[0, N_GROUPS:N_GROUPS + N_EXPERTS].set(moe_b_expert[l])
        xp, h2_all, ids_p, wts_p = _post_mixer(o_p, w_out, xp, mod_p, l, ln_g[l, 0], ln_b[l, 0], w_router, b_router,
                                               tm=tm_p, alpha=alpha, h2_all=None, row0=0, n_all=n_all)
        xs, h2_all, ids_s, wts_s = _post_mixer(o_s, w_out, xs, mod_s, l, ln_g[l, 0], ln_b[l, 0], w_router, b_router,
                                               tm=tm_s, alpha=alpha, h2_all=h2_all, row0=mp, n_all=n_all)
        ids = jnp.concatenate([ids_p[:, :2], ids_s[:, :2]], axis=0)
        wts = jnp.concatenate([wts_p[:, :2], wts_s[:, :2]], axis=0)
        dest, slot_tok, slot_w, block_expert, n_used = _moe_dispatch(ids, wts, MOE_ROWS)
        y_slots = _moe_experts(h2_all, slot_tok, slot_w, block_expert, n_used, moe_w1, moe_w3, moe_w2, l)
        has_next = l + 1 < depth
        xp, hp = _combine(xp, y_slots, dest[:2 * mp], mod_p, l, ln_g[l, 1], ln_b[l, 1], tm=tm_p, alpha=alpha,
                          has_next=has_next)
        xs, hs = _combine(xs, y_slots, dest[2 * mp:], mod_s, l, ln_g[l, 1], ln_b[l, 1], tm=tm_s, alpha=alpha,
                          has_next=has_next)

    kv_shape = (n_att, n_batch, seq, N_HEADS, HEAD_DIM)
    return (xp.reshape(n_batch, seq, d), xs.reshape(n_dec, t_new, d),
            kp_buf.reshape(kv_shape), vp_buf.reshape(kv_shape), jnp.stack(conv_p),
            jnp.stack(ks_rows), jnp.stack(vs_rows), jnp.stack(conv_s))
```

```python
import functools
import math

import jax
import jax.numpy as jnp
from jax import lax
from jax.experimental import pallas as pl
from jax.experimental.pallas import tpu as pltpu

F32, BF16, I32 = jnp.float32, jnp.bfloat16, jnp.int32
HIGHEST = lax.Precision.HIGHEST

HEAD_DIM = 128
HALF_DIM = HEAD_DIM // 2
N_HEADS_A = 8
N_HEADS_B = 8
N_HEADS = N_HEADS_A + N_HEADS_B
MOBA_BLOCK = 256
MOBA_TOPK = 3
REL_BUCKETS = 32
REL_MAX_DIST = 128
N_GROUPS = 4
EXPERTS_PER_GROUP = 8
N_EXPERTS = N_GROUPS * EXPERTS_PER_GROUP
TOP_EXPERT = 2
CONV_WIDTH = 3
LN_EPS = 1e-5
NEG_INF = -1e30

LANES = 128
ATT_TILE = MOBA_BLOCK
ROW_TILE = 256
MOE_ROWS = 256
PAGES_PER_STEP = 4
VMEM_LIMIT = 56 << 20


def _cparams(n_axes, vmem=VMEM_LIMIT):
    return pltpu.CompilerParams(dimension_semantics=("arbitrary",) * n_axes, vmem_limit_bytes=vmem)


def _smem_spec():
    return pl.BlockSpec(memory_space=pltpu.SMEM)


def _layer_norm(y, g, b):
    mu = jnp.mean(y, axis=-1, keepdims=True)
    yc = y - mu
    var = jnp.mean(yc * yc, axis=-1, keepdims=True)
    return yc * lax.rsqrt(var + LN_EPS) * g + b


def _silu(x):
    return x * (1.0 / (1.0 + jnp.exp(-x)))


def _dot_nt(a, b):
    return lax.dot_general(a, b, (((1,), (1,)), ((), ())), preferred_element_type=F32)


def _adaln_kernel(c_ref, w_ref, b_ref, o_ref):
    s = _silu(c_ref[...])
    o_ref[...] = jnp.dot(s, w_ref[...], precision=HIGHEST, preferred_element_type=F32) + b_ref[...]


def _adaln(c_all, ada_w, ada_b):
    depth, d, n = ada_w.shape
    rows = c_all.shape[0]
    tn = 1024
    return pl.pallas_call(
        _adaln_kernel,
        grid=(depth, n // tn),
        in_specs=[pl.BlockSpec((rows, d), lambda l, j: (0, 0)),
                  pl.BlockSpec((None, d, tn), lambda l, j: (l, 0, j)),
                  pl.BlockSpec((None, 1, tn), lambda l, j: (l, 0, j))],
        out_specs=pl.BlockSpec((None, rows, tn), lambda l, j: (l, 0, j)),
        out_shape=jax.ShapeDtypeStruct((depth, rows, n), F32),
        compiler_params=_cparams(2),
        name="adaln",
    )(c_all, ada_w, ada_b.reshape(depth, 1, n))


class _Mod:
    def __init__(self, table, rows_per_seq, row_tile, per_row):
        self.table, self.rows_per_seq, self.row_tile, self.per_row = table, rows_per_seq, row_tile, per_row

    def spec(self, layer, which, tn=None, row_arg=0, col_arg=None):
        d = self.table.shape[-1]
        tn = d if tn is None else tn
        tiles_per_seq = max(self.rows_per_seq // self.row_tile, 1)
        n_seq = self.table.shape[1] if not self.per_row else None

        def index(*g):
            col = 0 if col_arg is None else g[col_arg]
            if self.per_row:
                return (layer * 6 + which, g[row_arg], col)
            return ((layer * n_seq + g[row_arg] // tiles_per_seq) * 6 + which, 0, col)

        rows = self.row_tile if self.per_row else 1
        return pl.BlockSpec((None, rows, tn), index)

    @property
    def array(self):
        if self.per_row:
            return self.table
        depth, n_seq, six, d = self.table.shape
        return self.table.reshape(depth * n_seq * six, 1, d)


def _vec_spec(d):
    return pl.BlockSpec((1, d), lambda *g: (0, 0))


def _modulate_kernel(x_ref, shift_ref, scale_ref, h_ref):
    h_ref[...] = (x_ref[...] * (1.0 + scale_ref[...]) + shift_ref[...]).astype(h_ref.dtype)


def _modulate(x, mod, layer, tm):
    m, d = x.shape
    return pl.pallas_call(
        _modulate_kernel,
        grid=(m // tm,),
        in_specs=[pl.BlockSpec((tm, d), lambda i: (i, 0)), mod.spec(layer, 0), mod.spec(layer, 1)],
        out_specs=pl.BlockSpec((tm, d), lambda i: (i, 0)),
        out_shape=jax.ShapeDtypeStruct((m, d), BF16),
        compiler_params=_cparams(1),
        name="modulate",
    )(x, mod.array, mod.array)


def _matmul_kernel(x_ref, w_ref, *refs, n_alias):
    acc = jnp.dot(x_ref[...], w_ref[...], preferred_element_type=F32)
    for o_ref in refs[n_alias:]:
        o_ref[...] = acc.astype(o_ref.dtype)


def _project(x, w, *, tm, tn, out_dtypes, stacked=None, layer=0, n_stack=1):
    m, k = x.shape
    n = w.shape[1]
    in_specs = [pl.BlockSpec((tm, k), lambda i, j: (i, 0)), pl.BlockSpec((k, tn), lambda i, j: (0, j))]
    args = [x, w]
    out_specs, out_shape, aliases = [], [], {}
    n_alias = 0
    for dt in out_dtypes:
        if dt == F32 and stacked is not None:
            out_specs.append(pl.BlockSpec((None, tm, tn), lambda i, j: (layer, i, j)))
            out_shape.append(jax.ShapeDtypeStruct((n_stack, m, n), F32))
            if stacked is not True:
                in_specs.append(pl.BlockSpec(memory_space=pl.ANY))
                args.append(stacked)
                aliases[len(args) - 1] = len(out_shape) - 1
                n_alias = 1
        else:
            out_specs.append(pl.BlockSpec((tm, tn), lambda i, j: (i, j)))
            out_shape.append(jax.ShapeDtypeStruct((m, n), dt))
    return pl.pallas_call(
        functools.partial(_matmul_kernel, n_alias=n_alias),
        grid=(m // tm, n // tn),
        in_specs=in_specs, out_specs=out_specs, out_shape=out_shape,
        input_output_aliases=aliases,
        compiler_params=_cparams(2),
        name="project",
    )(*args)


def _t5_bucket(n):
    n = jnp.maximum(n, 0)
    max_exact = REL_BUCKETS // 2
    nf = jnp.maximum(n, 1).astype(F32)
    large = max_exact + (jnp.log(nf / max_exact) / math.log(REL_MAX_DIST / max_exact)
                         * (REL_BUCKETS - max_exact)).astype(I32)
    return jnp.where(n < max_exact, n, jnp.minimum(large, REL_BUCKETS - 1))


def _bias_of_distance(table, dist):
    bucket = _t5_bucket(dist)
    out = jnp.zeros((table.shape[1],) + bucket.shape, F32)
    for b in range(REL_BUCKETS):
        out = out + jnp.where(bucket == b, table[b].reshape((-1,) + (1,) * bucket.ndim), 0.0)
    return out


def _prompt_bias_tiles(table, t):
    assert t >= REL_MAX_DIST
    kk = jnp.arange(t)[:, None]
    qq = jnp.arange(t)[None, :]
    d0 = qq - kk
    same = jnp.where(d0 >= 0, _bias_of_distance(table, d0), NEG_INF)
    prev = _bias_of_distance(table, d0 + t)
    far = jnp.broadcast_to(table[REL_BUCKETS - 1][:, None, None], prev.shape)
    return jnp.stack([same, prev, far], axis=1).astype(F32)


def _transpose_v(v_ref, vt_ref, t):
    for c in range(v_ref.shape[0] // t):
        vt_ref[c] = v_ref[c * t:(c + 1) * t, :].astype(F32).T.astype(BF16)


def _softmax_block(s, vt, m_ref, l_ref, acc_ref, idx):
    m_old = m_ref[idx]
    m_new = jnp.maximum(m_old, jnp.max(s, axis=0, keepdims=True))
    a = jnp.exp(m_old - m_new)
    p = jnp.exp(s - m_new)
    l_ref[idx] = a * l_ref[idx] + jnp.sum(p, axis=0, keepdims=True)
    acc_ref[idx] = a * acc_ref[idx] + jnp.dot(vt, p.astype(BF16), preferred_element_type=F32)
    m_ref[idx] = m_new


def _init_stats(m_ref, l_ref, acc_ref):
    m_ref[...] = jnp.full(m_ref.shape, NEG_INF, F32)
    l_ref[...] = jnp.zeros(l_ref.shape, F32)
    acc_ref[...] = jnp.zeros(acc_ref.shape, F32)


def _diff_attn_kernel(lam_ref, q_ref, k_ref, v_ref, bias_ref, g_ref, o_ref,
                      vt_ref, m_ref, l_ref, acc_ref, *, t, out_scale):
    qi = pl.program_id(2)

    @pl.when(qi == 0)
    def _():
        _transpose_v(v_ref, vt_ref, t)

    q = q_ref[...].astype(F32) * (HALF_DIM ** -0.5)
    lane = lax.broadcasted_iota(I32, q.shape, 1)
    qs = (jnp.where(lane < HALF_DIM, q, 0.0).astype(BF16), jnp.where(lane >= HALF_DIM, q, 0.0).astype(BF16))
    _init_stats(m_ref, l_ref, acc_ref)

    def block(j, bias):
        kb = k_ref[pl.ds(pl.multiple_of(j * t, t), t), :]
        vt = vt_ref[j]
        for idx in range(2):
            _softmax_block(_dot_nt(kb, qs[idx]) + bias, vt, m_ref, l_ref, acc_ref, idx)

    block(qi, bias_ref[0])

    def body(j, c):
        block(j, bias_ref[jnp.minimum(qi - j, 2)])
        return c

    lax.fori_loop(0, qi, body, 0)

    o = acc_ref[0] / l_ref[0] - lam_ref[0] * (acc_ref[1] / l_ref[1])
    ms = jnp.mean(o * o, axis=0, keepdims=True)
    o = o * lax.rsqrt(ms + LN_EPS) * g_ref[...] * out_scale
    o_ref[...] = o.T.astype(o_ref.dtype)


def _moba_attn_kernel(q_ref, k_ref, v_ref, bias_ref, o_in_ref, o_ref,
                      vt_ref, kmean_ref, sel_ref, m_ref, l_ref, acc_ref, *, t):
    del o_in_ref
    qi = pl.program_id(2)
    nb = k_ref.shape[0] // t

    @pl.when(qi == 0)
    def _():
        _transpose_v(v_ref, vt_ref, t)
        for c in range(nb):
            kmean_ref[c:c + 1, :] = jnp.mean(k_ref[c * t:(c + 1) * t, :].astype(F32), axis=0, keepdims=True)

    q = q_ref[...].astype(F32)
    gate = lax.dot_general(kmean_ref[...], q, (((1,), (1,)), ((), ())), precision=HIGHEST,
                           preferred_element_type=F32)
    blk = lax.broadcasted_iota(I32, gate.shape, 0)
    rank = jnp.zeros(gate.shape, F32)
    for n in range(nb):
        row = gate[n:n + 1, :]
        beats = jnp.logical_or(row > gate, jnp.logical_and(row == gate, n < blk))
        rank = rank + jnp.where(jnp.logical_and(beats, n < qi), 1.0, 0.0)
    sel_ref[...] = jnp.where(jnp.logical_and(blk < qi, rank < MOBA_TOPK), 1.0, 0.0)

    qs = (q * (HEAD_DIM ** -0.5)).astype(BF16)
    _init_stats(m_ref, l_ref, acc_ref)
    kb = k_ref[pl.ds(pl.multiple_of(qi * t, t), t), :]
    _softmax_block(_dot_nt(kb, qs) + bias_ref[0], vt_ref[qi], m_ref, l_ref, acc_ref, 0)

    def body(j, c):
        kb = k_ref[pl.ds(pl.multiple_of(j * t, t), t), :]
        s = _dot_nt(kb, qs) + bias_ref[jnp.minimum(qi - j, 2)]
        s = jnp.where(sel_ref[pl.ds(j, 1), :] > 0.0, s, NEG_INF)
        _softmax_block(s, vt_ref[j], m_ref, l_ref, acc_ref, 0)
        return c

    lax.fori_loop(0, qi, body, 0)
    o_ref[...] = (acc_ref[0] / l_ref[0]).T.astype(o_ref.dtype)


def _prompt_attention(q, k, v, tiles, lam, subln_g, lam_init, n_batch, seq):
    m, d_att = q.shape
    t = ATT_TILE
    assert seq % t == 0 and t == MOBA_BLOCK
    nq = seq // t
    nb = seq // t
    na = N_HEADS_A

    def specs(h0):
        return [pl.BlockSpec((t, HEAD_DIM), lambda b, h, i: (b * nq + i, h0 + h)),
                pl.BlockSpec((seq, HEAD_DIM), lambda b, h, i: (b, h0 + h)),
                pl.BlockSpec((seq, HEAD_DIM), lambda b, h, i: (b, h0 + h)),
                pl.BlockSpec((None, 3, t, t), lambda b, h, i: (h0 + h, 0, 0, 0))]

    stats = [pltpu.VMEM((2, 1, t), F32), pltpu.VMEM((2, 1, t), F32), pltpu.VMEM((2, HEAD_DIM, t), F32)]
    o = pl.pallas_call(
        functools.partial(_diff_attn_kernel, t=t, out_scale=1.0 - lam_init),
        grid=(n_batch, na, nq),
        in_specs=[_smem_spec()] + specs(0) + [pl.BlockSpec((HEAD_DIM, 1), lambda b, h, i: (0, 0))],
        out_specs=pl.BlockSpec((t, HEAD_DIM), lambda b, h, i: (b * nq + i, h)),
        out_shape=jax.ShapeDtypeStruct((m, d_att), BF16),
        scratch_shapes=[pltpu.VMEM((nb, HEAD_DIM, t), BF16)] + stats,
        compiler_params=_cparams(3),
        name="diff_attention",
    )(lam.reshape(1), q, k, v, tiles, subln_g.reshape(HEAD_DIM, 1))
    return pl.pallas_call(
        functools.partial(_moba_attn_kernel, t=t),
        grid=(n_batch, N_HEADS_B, nq),
        in_specs=specs(na) + [pl.BlockSpec(memory_space=pl.ANY)],
        out_specs=pl.BlockSpec((t, HEAD_DIM), lambda b, h, i: (b * nq + i, na + h)),
        out_shape=jax.ShapeDtypeStruct((m, d_att), BF16),
        scratch_shapes=[pltpu.VMEM((nb, HEAD_DIM, t), BF16), pltpu.VMEM((nb, HEAD_DIM), F32),
                        pltpu.VMEM((nb, t), F32)] + stats,
        input_output_aliases={4: 0},
        compiler_params=_cparams(3),
        name="moba_attention",
    )(q, k, v, tiles, o)


def _decode_kernel(pt_ref, lam_ref, q_ref, knew_ref, vnew_ref, far_ref, dlast_ref, bnew_ref, g_ref, *refs,
                   n_pages_step, n_steps, n_blocks, out_scale):
    del pt_ref
    p_step = n_pages_step
    k_pages = refs[:p_step]
    v_pages = refs[p_step:2 * p_step]
    o_ref = refs[2 * p_step]
    (qrows_ref, s_ref, p_ref, pv_ref, kpad_ref, vpad_ref, m_ref, l_ref, acc_ref,
     gate_ref, pm_ref, pl_ref, po_ref) = refs[2 * p_step + 1:]
    step = pl.program_id(1)
    t_new = q_ref.shape[0]
    hr = 2 * t_new
    half = N_HEADS_A * hr
    page = LANES
    pages_per_blk = MOBA_BLOCK // page
    blks_per_step = p_step // pages_per_blk
    na = N_HEADS_A

    def rows(h):
        return slice(h * hr, (h + 1) * hr)

    def cols(h):
        return slice(h * HEAD_DIM, (h + 1) * HEAD_DIM)

    def page_head(ref, h):
        return ref[pl.ds(h, page, stride=N_HEADS), :].astype(BF16)

    @pl.when(step == 0)
    def _():
        for h in range(N_HEADS):
            qh = q_ref[:, cols(h)]
            if h < na:
                qh = qh * (HALF_DIM ** -0.5)
                lane = lax.broadcasted_iota(I32, qh.shape, 1)
                blk = jnp.concatenate([jnp.where(lane < HALF_DIM, qh, 0.0), jnp.where(lane >= HALF_DIM, qh, 0.0)],
                                      axis=0)
            else:
                blk = jnp.concatenate([qh * (HEAD_DIM ** -0.5), jnp.zeros_like(qh)], axis=0)
            qrows_ref[rows(h), :] = blk.astype(BF16)
        _init_stats(m_ref, l_ref, acc_ref)
        gate_ref[...] = jnp.zeros(gate_ref.shape, F32)
        pm_ref[...] = jnp.zeros(pm_ref.shape, F32)
        pl_ref[...] = jnp.zeros(pl_ref.shape, F32)

    def diff_softmax(s, width):
        m_old = m_ref[...]
        m_new = jnp.maximum(m_old, jnp.max(s, axis=1, keepdims=True))
        a = jnp.exp(m_old - m_new)
        p = jnp.exp(s - m_new)
        l_ref[...] = a * l_ref[...] + jnp.sum(p, axis=1, keepdims=True)
        m_ref[...] = m_new
        p_ref[0:half, 0:width] = p.astype(BF16)
        return a

    for h in range(N_HEADS):
        qh = qrows_ref[rows(h), :]
        for u in range(p_step):
            s_ref[rows(h), u * page:(u + 1) * page] = _dot_nt(qh, page_head(k_pages[u], h))

    last_f = jnp.where(step == n_steps - 1, 1.0, 0.0)
    bias = far_ref[...] + dlast_ref[...] * last_f
    a = diff_softmax(s_ref[0:half, :] + bias[0:half], p_step * page)
    raw = s_ref[half:, :]
    sb = raw + bias[half:]
    blk_lane = lax.broadcasted_iota(I32, gate_ref.shape, 1)
    for c in range(blks_per_step):
        kc = slice(c * MOBA_BLOCK, (c + 1) * MOBA_BLOCK)
        hit = blk_lane == step * blks_per_step + c
        m_c = jnp.max(sb[:, kc], axis=1, keepdims=True)
        p_c = jnp.exp(sb[:, kc] - m_c)
        gate_ref[...] = jnp.where(hit, jnp.sum(raw[:, kc], axis=1, keepdims=True), gate_ref[...])
        pm_ref[...] = jnp.where(hit, m_c, pm_ref[...])
        pl_ref[...] = jnp.where(hit, jnp.sum(p_c, axis=1, keepdims=True), pl_ref[...])
        p_ref[half:, kc] = p_c.astype(BF16)

    def pv(h, us):
        ph = p_ref[rows(h), :]
        return sum(jnp.dot(ph[:, u * page:(u + 1) * page], page_head(v_pages[u], h), preferred_element_type=F32)
                   for u in us)

    for h in range(na):
        pv_ref[rows(h), :] = pv(h, range(p_step))
    for hb in range(N_HEADS_B):
        for c in range(blks_per_step):
            po_ref[step * blks_per_step + c, rows(hb), :] = pv(na + hb, range(c * pages_per_blk,
                                                                                (c + 1) * pages_per_blk))
    acc_ref[...] = a * acc_ref[...] + pv_ref[0:half, :]

    @pl.when(step == n_steps - 1)
    def _():
        kpad_ref[...] = jnp.zeros(kpad_ref.shape, F32)
        vpad_ref[...] = jnp.zeros(vpad_ref.shape, F32)
        kpad_ref[0:t_new, :] = knew_ref[...]
        vpad_ref[0:t_new, :] = vnew_ref[...]
        for h in range(N_HEADS):
            s_ref[rows(h), 0:page] = _dot_nt(qrows_ref[rows(h), :], kpad_ref[:, cols(h)].astype(BF16))
        sn = s_ref[:, 0:page] + bnew_ref[...]
        a_new = diff_softmax(sn[0:half], page)
        m_own = jnp.max(sn[half:], axis=1, keepdims=True)
        p_own = jnp.exp(sn[half:] - m_own)
        l_own = jnp.sum(p_own, axis=1, keepdims=True)
        p_ref[half:, 0:page] = p_own.astype(BF16)
        for h in range(N_HEADS):
            pv_ref[rows(h), :] = jnp.dot(p_ref[rows(h), 0:page], vpad_ref[:, cols(h)].astype(BF16),
                                         preferred_element_type=F32)
        o = (a_new * acc_ref[...] + pv_ref[0:half, :]) / l_ref[...]
        lam = lam_ref[0]
        for h in range(na):
            od = o[h * hr:h * hr + t_new] - lam * o[h * hr + t_new:(h + 1) * hr]
            ms = jnp.mean(od * od, axis=1, keepdims=True)
            o_ref[:, cols(h)] = od * lax.rsqrt(ms + LN_EPS) * g_ref[...] * out_scale

        lane_f = blk_lane.astype(F32)
        gate = jnp.where(blk_lane < n_blocks, gate_ref[...], -jnp.inf)
        sel = jnp.zeros(gate.shape, jnp.bool_)
        for _ in range(min(MOBA_TOPK, n_blocks)):
            best = jnp.max(gate, axis=1, keepdims=True)
            first = jnp.min(jnp.where(gate == best, lane_f, float(LANES)), axis=1, keepdims=True)
            pick = lane_f == first
            sel = jnp.logical_or(sel, pick)
            gate = jnp.where(pick, -jnp.inf, gate)
        pm = pm_ref[...]
        m_all = jnp.maximum(m_own, jnp.max(jnp.where(sel, pm, NEG_INF), axis=1, keepdims=True))
        w = jnp.where(sel, jnp.exp(jnp.where(sel, pm, NEG_INF) - m_all), 0.0)
        w_own = jnp.exp(m_own - m_all)
        l_all = w_own * l_own + jnp.sum(w * pl_ref[...], axis=1, keepdims=True)
        o_all = w_own * pv_ref[half:, :]
        for n in range(n_blocks):
            o_all = o_all + w[:, n:n + 1] * po_ref[n]
        o_all = o_all / l_all
        for hb in range(N_HEADS_B):
            o_ref[:, cols(na + hb)] = o_all[hb * hr:hb * hr + t_new]


def _sample_attention(q, k_new, v_new, cache_k, cache_v, page_table, layer, table, lam, subln_g, lam_init):
    n_seq, n_pages = page_table.shape
    t_new = q.shape[0] // n_seq
    d_att = q.shape[1]
    page = cache_k.shape[2]
    past = n_pages * page
    p_step = PAGES_PER_STEP
    assert page == LANES and n_pages % p_step == 0 and past % MOBA_BLOCK == 0 and p_step % (MOBA_BLOCK // page) == 0
    assert page >= REL_MAX_DIST and t_new == 8 and past // MOBA_BLOCK <= LANES
    n_steps = n_pages // p_step
    n_blocks = past // MOBA_BLOCK
    hr = 2 * t_new
    n_rows = N_HEADS * hr
    half = N_HEADS_A * hr
    ck = cache_k.reshape(cache_k.shape[0], cache_k.shape[1], page * N_HEADS, HEAD_DIM)
    cv = cache_v.reshape(cache_v.shape[0], cache_v.shape[1], page * N_HEADS, HEAD_DIM)

    tt = jnp.arange(t_new)[:, None]
    rr = jnp.arange(page)[None, :]
    far = table[REL_BUCKETS - 1]
    dlast = _bias_of_distance(table, page + tt - rr) - far[:, None, None]
    dnew = tt - rr
    bnew = jnp.where(jnp.logical_and(dnew >= 0, rr < t_new), _bias_of_distance(table, dnew), NEG_INF)
    is_diff = (jnp.arange(N_HEADS) < N_HEADS_A)[:, None, None]
    dlast = jnp.concatenate([dlast, jnp.where(is_diff, dlast, 0.0)], axis=1).reshape(n_rows, page)
    bnew = jnp.concatenate([bnew, jnp.where(is_diff, bnew, 0.0)], axis=1).reshape(n_rows, page)
    dlast = jnp.concatenate([jnp.zeros((n_rows, (p_step - 1) * page), F32), dlast.astype(F32)], axis=1)
    far_rows = jnp.repeat(far.astype(F32), hr).reshape(n_rows, 1)

    def page_spec(u):
        return pl.BlockSpec((None, None, page * N_HEADS, HEAD_DIM),
                            lambda b, s, pt: (layer, pt[b * n_pages + s * p_step + u], 0, 0))

    def full(shape):
        return pl.BlockSpec(shape, lambda b, s, pt: (0,) * len(shape))

    row_spec = pl.BlockSpec((t_new, d_att), lambda b, s, pt: (b, 0))
    grid_spec = pltpu.PrefetchScalarGridSpec(
        num_scalar_prefetch=1,
        grid=(n_seq, n_steps),
        in_specs=[_smem_spec(), row_spec, row_spec, row_spec, full((n_rows, 1)), full((n_rows, p_step * page)),
                  full((n_rows, page)), full((1, HEAD_DIM))]
                 + [page_spec(u) for u in range(p_step)] * 2,
        out_specs=row_spec,
        scratch_shapes=[pltpu.VMEM((n_rows, HEAD_DIM), BF16), pltpu.VMEM((n_rows, p_step * page), F32),
                        pltpu.VMEM((n_rows, p_step * page), BF16), pltpu.VMEM((n_rows, HEAD_DIM), F32),
                        pltpu.VMEM((page, d_att), F32), pltpu.VMEM((page, d_att), F32),
                        pltpu.VMEM((half, 1), F32), pltpu.VMEM((half, 1), F32), pltpu.VMEM((half, HEAD_DIM), F32),
                        pltpu.VMEM((n_rows - half, LANES), F32), pltpu.VMEM((n_rows - half, LANES), F32),
                        pltpu.VMEM((n_rows - half, LANES), F32),
                        pltpu.VMEM((n_blocks, n_rows - half, HEAD_DIM), F32)],
    )
    return pl.pallas_call(
        functools.partial(_decode_kernel, n_pages_step=p_step, n_steps=n_steps, n_blocks=n_blocks,
                          out_scale=1.0 - lam_init),
        grid_spec=grid_spec,
        out_shape=jax.ShapeDtypeStruct(q.shape, F32),
        compiler_params=_cparams(2),
        name="decode_attention",
    )(page_table.reshape(-1), lam.reshape(1), q, k_new, v_new, far_rows, dlast, bnew.astype(F32),
      subln_g.reshape(1, HEAD_DIM), *([ck] * p_step), *([cv] * p_step))


def _conv_in_kernel(x_ref, wb_ref, wc_ref, wv_ref, cw_ref, st_ref, z_ref, tail_ref, carry_ref, *, tiles_per_seq):
    i = pl.program_id(1)
    x = x_ref[...]
    gb = jnp.dot(x, wb_ref[...], preferred_element_type=F32)
    u = jnp.dot(x, wc_ref[...], preferred_element_type=F32) * jnp.dot(x, wv_ref[...], preferred_element_type=F32)
    tm = u.shape[0]

    @pl.when(i % tiles_per_seq == 0)
    def _():
        carry_ref[...] = st_ref[...]

    prev = carry_ref[...]
    row = lax.broadcasted_iota(I32, u.shape, 0)
    u1 = jnp.where(row == 0, prev[1:2], pltpu.roll(u, 1, axis=0))
    u2 = jnp.where(row == 0, prev[0:1], jnp.where(row == 1, prev[1:2], pltpu.roll(u, 2, axis=0)))
    cw = cw_ref[...]
    z_ref[...] = (gb * (cw[0:1] * u2 + cw[1:2] * u1 + cw[2:3] * u)).astype(z_ref.dtype)
    tail = u[tm - (CONV_WIDTH - 1):tm]
    carry_ref[...] = tail
    tail_ref[...] = tail


def _conv_in(h, w_in, conv_w, state, *, rows_per_seq, tm, tn):
    m, d = h.shape
    n_seq = m // rows_per_seq
    assert rows_per_seq % tm == 0 and tm >= CONV_WIDTH - 1
    tiles_per_seq = rows_per_seq // tm
    nj = d // tn
    st_spec = pl.BlockSpec((None, CONV_WIDTH - 1, tn), lambda j, i: (i // tiles_per_seq, 0, j))
    return pl.pallas_call(
        functools.partial(_conv_in_kernel, tiles_per_seq=tiles_per_seq),
        grid=(nj, m // tm),
        in_specs=[pl.BlockSpec((tm, d), lambda j, i: (i, 0))] + _w_in_specs(d, tn, nj, lambda j, i: j)
                 + [pl.BlockSpec((CONV_WIDTH, tn), lambda j, i: (0, j)), st_spec],
        out_specs=[pl.BlockSpec((tm, tn), lambda j, i: (i, j)), st_spec],
        out_shape=[jax.ShapeDtypeStruct((m, d), BF16), jax.ShapeDtypeStruct((n_seq, CONV_WIDTH - 1, d), F32)],
        scratch_shapes=[pltpu.VMEM((CONV_WIDTH - 1, tn), F32)],
        compiler_params=_cparams(2),
        name="conv_in",
    )(h, w_in, w_in, w_in, conv_w, state)


def _w_in_specs(d, tn, nj, col_of):
    def spec(part):
        return pl.BlockSpec((d, tn), lambda *g: (0, part * nj + col_of(*g)))
    return [spec(part) for part in range(3)]


def _conv_in_short_kernel(x_ref, wb_ref, wc_ref, wv_ref, cw_ref, p0_ref, p1_ref, z_ref, u_ref, *, t):
    x = x_ref[...]
    gb = jnp.dot(x, wb_ref[...], preferred_element_type=F32)
    u = jnp.dot(x, wc_ref[...], preferred_element_type=F32) * jnp.dot(x, wv_ref[...], preferred_element_type=F32)
    pos = lax.rem(lax.broadcasted_iota(I32, u.shape, 0), t)
    u1 = jnp.where(pos == 0, p1_ref[...], pltpu.roll(u, 1, axis=0))
    u2 = jnp.where(pos == 0, p0_ref[...], jnp.where(pos == 1, p1_ref[...], pltpu.roll(u, 2, axis=0)))
    cw = cw_ref[...]
    z_ref[...] = (gb * (cw[0:1] * u2 + cw[1:2] * u1 + cw[2:3] * u)).astype(z_ref.dtype)
    u_ref[...] = u


def _conv_in_short(h, w_in, conv_w, state, *, t, tn):
    m, d = h.shape
    n_seq = m // t
    assert t >= CONV_WIDTH - 1
    nj = d // tn
    p0 = jnp.repeat(state[:, 0, :], t, axis=0)
    p1 = jnp.repeat(state[:, 1, :], t, axis=0)
    col = pl.BlockSpec((m, tn), lambda j: (0, j))
    z, u = pl.pallas_call(
        functools.partial(_conv_in_short_kernel, t=t),
        grid=(nj,),
        in_specs=[pl.BlockSpec((m, d), lambda j: (0, 0))] + _w_in_specs(d, tn, nj, lambda j: j)
                 + [pl.BlockSpec((CONV_WIDTH, tn), lambda j: (0, j)), col, col],
        out_specs=[col, col],
        out_shape=[jax.ShapeDtypeStruct((m, d), BF16), jax.ShapeDtypeStruct((m, d), F32)],
        compiler_params=_cparams(1),
        name="conv_in_short",
    )(h, w_in, w_in, w_in, conv_w, p0, p1)
    return z, u.reshape(n_seq, t, d)[:, t - (CONV_WIDTH - 1):]


def _route(logits):
    lane = lax.broadcasted_iota(I32, logits.shape, 1).astype(F32)
    g_mask = lane < N_GROUPS
    g_exp = jnp.exp(jnp.where(g_mask, logits, -jnp.inf)
                    - jnp.max(jnp.where(g_mask, logits, -jnp.inf), axis=1, keepdims=True))
    g_prob = g_exp / jnp.sum(g_exp, axis=1, keepdims=True)
    g_p = jnp.max(g_prob, axis=1, keepdims=True)
    g_idx = jnp.min(jnp.where(jnp.logical_and(g_prob == g_p, g_mask), lane, float(LANES)), axis=1, keepdims=True)
    lo = N_GROUPS + EXPERTS_PER_GROUP * g_idx
    e_mask = jnp.logical_and(lane >= lo, lane < lo + EXPERTS_PER_GROUP)
    e_logit = jnp.where(e_mask, logits, -jnp.inf)
    e_exp = jnp.exp(e_logit - jnp.max(e_logit, axis=1, keepdims=True))
    e_prob = jnp.where(e_mask, e_exp / jnp.sum(e_exp, axis=1, keepdims=True), -1.0)
    p1 = jnp.max(e_prob, axis=1, keepdims=True)
    i1 = jnp.min(jnp.where(e_prob == p1, lane, float(LANES)), axis=1, keepdims=True)
    rest = jnp.where(lane == i1, -1.0, e_prob)
    p2 = jnp.max(rest, axis=1, keepdims=True)
    i2 = jnp.min(jnp.where(rest == p2, lane, float(LANES)), axis=1, keepdims=True)
    tot = p1 + p2
    ids = jnp.where(lane == 0.0, i1 - N_GROUPS, jnp.where(lane == 1.0, i2 - N_GROUPS, 0.0)).astype(I32)
    wts = jnp.where(lane == 0.0, g_p * (p1 / tot), jnp.where(lane == 1.0, g_p * (p2 / tot), 0.0))
    return ids, wts


def _store_token_major(ref, x):
    rows, width = x.shape
    s = width // LANES
    for c in range(s):
        ref[pl.ds(c, rows, stride=s), :] = x[:, c * LANES:(c + 1) * LANES]


def _load_token_major(ref, rows, s):
    return jnp.concatenate([ref[pl.ds(c, rows, stride=s), :] for c in range(s)], axis=1)


def _token_copy(src, tok, dst, slot, s, sem):
    return pltpu.make_async_copy(src.at[pl.ds(pl.multiple_of(tok * s, s), s), :],
                                 dst.at[pl.ds(pl.multiple_of(slot * s, s), s), :], sem)


def _post_mixer_kernel(o_ref, w_ref, x_ref, gate_ref, lng_ref, lnb_ref, shift_ref, scale_ref, wr_ref, br_ref,
                       *refs, alpha, n_alias):
    x1_ref, h2_ref, ids_ref, wts_ref = refs[n_alias:]
    f = jnp.dot(o_ref[...].astype(BF16), w_ref[...], preferred_element_type=F32)
    x1 = _layer_norm(alpha * x_ref[...] + (1.0 + gate_ref[...]) * f, lng_ref[...], lnb_ref[...])
    x1_ref[...] = x1
    h2 = x1 * (1.0 + scale_ref[...]) + shift_ref[...]
    _store_token_major(h2_ref, h2)
    logits = jnp.dot(h2, wr_ref[...], precision=HIGHEST, preferred_element_type=F32) + br_ref[...]
    ids_ref[...], wts_ref[...] = _route(logits)


def _post_mixer(o, w, x, mod, layer, ln_g, ln_b, w_router, b_router, *, tm, alpha, h2_all, row0, n_all):
    m, d = x.shape
    k = o.shape[1]
    assert row0 % tm == 0
    blk0 = row0 // tm
    row = lambda i: (i, 0)
    in_specs = [pl.BlockSpec((tm, k), row), pl.BlockSpec((k, d), lambda i: (0, 0)), pl.BlockSpec((tm, d), row),
                mod.spec(layer, 2), _vec_spec(d), _vec_spec(d), mod.spec(layer, 3), mod.spec(layer, 4),
                pl.BlockSpec((d, LANES), lambda i: (0, 0)), _vec_spec(LANES)]
    args = [o, w, x, mod.array, ln_g.reshape(1, d), ln_b.reshape(1, d), mod.array, mod.array, w_router, b_router]
    aliases, n_alias = {}, 0
    if h2_all is not None:
        in_specs.append(pl.BlockSpec(memory_space=pl.ANY))
        args.append(h2_all)
        aliases, n_alias = {len(args) - 1: 1}, 1
    return pl.pallas_call(
        functools.partial(_post_mixer_kernel, alpha=alpha, n_alias=n_alias),
        grid=(m // tm,),
        in_specs=in_specs,
        out_specs=[pl.BlockSpec((tm, d), row), pl.BlockSpec((tm * (d // LANES), LANES), lambda i: (blk0 + i, 0)),
                   pl.BlockSpec((tm, LANES), row), pl.BlockSpec((tm, LANES), row)],
        out_shape=[jax.ShapeDtypeStruct((m, d), F32), jax.ShapeDtypeStruct((n_all * (d // LANES), LANES), F32),
                   jax.ShapeDtypeStruct((m, LANES), I32), jax.ShapeDtypeStruct((m, LANES), F32)],
        input_output_aliases=aliases,
        compiler_params=_cparams(1),
        name="post_mixer",
    )(*args)


def _moe_kernel(be_ref, dest_ref, nused_ref, h_hbm, w1_ref, w3_ref, w2_ref, y_ref,
                slot_ref, xbuf, w1b, w3b, w2b, sem, *, rows, chunks, top_k):
    i = pl.program_id(0)
    n_used = nused_ref[0]

    def gather(block, buf):
        base = block * rows

        def issue(r, c):
            _token_copy(h_hbm, lax.div(slot_ref[base + r], top_k), xbuf.at[buf], r, chunks, sem.at[buf]).start()
            return c

        lax.fori_loop(0, rows, issue, 0, unroll=8)

    @pl.when(i == 0)
    def _():
        def clear(s, c):
            slot_ref[s] = 0
            return c

        lax.fori_loop(0, slot_ref.shape[0], clear, 0, unroll=8)

        def place(a, c):
            slot_ref[dest_ref[a]] = a
            return c

        lax.fori_loop(0, dest_ref.shape[0], place, 0, unroll=8)
        gather(0, 0)

    @pl.when(i < n_used)
    def _():
        buf = lax.rem(i, 2)

        @pl.when(i + 1 < n_used)
        def _():
            gather(i + 1, 1 - buf)

        @pl.when(jnp.logical_or(i == 0, be_ref[i] != be_ref[jnp.maximum(i - 1, 0)]))
        def _():
            w1b[...] = w1_ref[...].astype(BF16)
            w3b[...] = w3_ref[...].astype(BF16)
            w2b[...] = w2_ref[...].astype(BF16)

        pltpu.make_async_copy(h_hbm.at[pl.ds(0, rows * chunks), :], xbuf.at[buf], sem.at[buf]).wait()
        x = _load_token_major(xbuf.at[buf], rows, chunks).astype(BF16)
        a = jnp.dot(x, w1b[...], preferred_element_type=F32)
        b = jnp.dot(x, w3b[...], preferred_element_type=F32)
        y = jnp.dot((_silu(a) * b).astype(BF16), w2b[...], preferred_element_type=F32)
        _store_token_major(y_ref, y)

    @pl.when(i >= n_used)
    def _():
        y_ref[...] = jnp.zeros(y_ref.shape, F32)


def _moe_dispatch(ids, rows):
    n_tok, k = ids.shape
    nk = n_tok * k
    flat_e = ids.reshape(nk)
    onehot = (flat_e[:, None] == jnp.arange(N_EXPERTS, dtype=I32)[None, :]).astype(I32)
    csum = jnp.cumsum(onehot, axis=0)
    counts = csum[-1]
    pos = jnp.sum((csum - onehot) * onehot, axis=1)
    padded = (counts + rows - 1) // rows * rows
    pad_end = jnp.cumsum(padded)
    pad_start = jnp.sum(jnp.where(onehot > 0, (pad_end - padded)[None, :], 0), axis=1)
    dest = (pad_start + pos).astype(I32)
    n_blocks = (nk + rows - 1) // rows + N_EXPERTS
    starts = jnp.arange(n_blocks, dtype=I32) * rows
    block_expert = jnp.minimum(jnp.sum((pad_end[None, :] <= starts[:, None]).astype(I32), axis=1),
                               N_EXPERTS - 1).astype(I32)
    n_used = (pad_end[-1] // rows).astype(I32).reshape(1)
    return dest, block_expert, n_used


def _moe_experts(h_all, dest, block_expert, n_used, w1, w3, w2, layer, top_k):
    rows = MOE_ROWS
    n_blocks = block_expert.shape[0]
    d, d_e = w1.shape[-2:]
    chunks = d // LANES
    grid_spec = pltpu.PrefetchScalarGridSpec(
        num_scalar_prefetch=3,
        grid=(n_blocks,),
        in_specs=[pl.BlockSpec(memory_space=pl.ANY),
                  pl.BlockSpec((None, None, d, d_e), lambda i, be, dst, nu: (layer, be[i], 0, 0)),
                  pl.BlockSpec((None, None, d, d_e), lambda i, be, dst, nu: (layer, be[i], 0, 0)),
                  pl.BlockSpec((None, None, d_e, d), lambda i, be, dst, nu: (layer, be[i], 0, 0))],
        out_specs=pl.BlockSpec((rows * chunks, LANES), lambda i, be, dst, nu: (i, 0)),
        scratch_shapes=[pltpu.SMEM((n_blocks * rows,), I32), pltpu.VMEM((2, rows * chunks, LANES), F32),
                        pltpu.VMEM((d, d_e), BF16), pltpu.VMEM((d, d_e), BF16), pltpu.VMEM((d_e, d), BF16),
                        pltpu.SemaphoreType.DMA((2,))],
    )
    return pl.pallas_call(
        functools.partial(_moe_kernel, rows=rows, chunks=chunks, top_k=top_k),
        grid_spec=grid_spec,
        out_shape=jax.ShapeDtypeStruct((n_blocks * rows * chunks, LANES), F32),
        compiler_params=_cparams(1),
        name="moe_experts",
    )(block_expert, dest, n_used, h_all, w1, w3, w2)


def _combine_kernel(dest_ref, x_ref, y_hbm, wts_ref, gate_ref, lng_ref, lnb_ref, *refs, tm, chunks, top_k, alpha,
                    has_next):
    if has_next:
        shift_ref, scale_ref, x2_ref, h_ref, ybuf, sem = refs
    else:
        x2_ref, ybuf, sem = refs
    base = pl.program_id(0) * (top_k * tm)

    def issue(r, c):
        for k in range(top_k):
            _token_copy(y_hbm, dest_ref[base + top_k * r + k], ybuf.at[k], r, chunks, sem).start()
        return c

    lax.fori_loop(0, tm, issue, 0, unroll=4)
    for k in range(top_k):
        pltpu.make_async_copy(y_hbm.at[pl.ds(0, tm * chunks), :], ybuf.at[k], sem).wait()
    wts = wts_ref[...]
    g = sum(wts[:, k:k + 1] * _load_token_major(ybuf.at[k], tm, chunks) for k in range(top_k))
    x2 = _layer_norm(alpha * x_ref[...] + (1.0 + gate_ref[...]) * g, lng_ref[...], lnb_ref[...])
    x2_ref[...] = x2
    if has_next:
        h_ref[...] = (x2 * (1.0 + scale_ref[...]) + shift_ref[...]).astype(h_ref.dtype)


def _combine(x1, y_slots, dest, wts, mod, layer, ln_g, ln_b, *, tm, alpha, has_next, top_k):
    m, d = x1.shape
    chunks = d // LANES
    row = lambda i, dst: (i, 0)
    vec = pl.BlockSpec((1, d), lambda i, dst: (0, 0))
    in_specs = [pl.BlockSpec((tm, d), row), pl.BlockSpec(memory_space=pl.ANY), pl.BlockSpec((tm, LANES), row),
                mod.spec(layer, 5), vec, vec]
    args = [x1, y_slots, wts, mod.array, ln_g.reshape(1, d), ln_b.reshape(1, d)]
    out_specs = [pl.BlockSpec((tm, d), row)]
    out_shape = [jax.ShapeDtypeStruct((m, d), F32)]
    if has_next:
        in_specs += [mod.spec(layer + 1, 0), mod.spec(layer + 1, 1)]
        args += [mod.array, mod.array]
        out_specs.append(pl.BlockSpec((tm, d), row))
        out_shape.append(jax.ShapeDtypeStruct((m, d), BF16))
    grid_spec = pltpu.PrefetchScalarGridSpec(
        num_scalar_prefetch=1, grid=(m // tm,), in_specs=in_specs, out_specs=out_specs,
        scratch_shapes=[pltpu.VMEM((top_k, tm * chunks, LANES), F32), pltpu.SemaphoreType.DMA(())])
    out = pl.pallas_call(
        functools.partial(_combine_kernel, tm=tm, chunks=chunks, top_k=top_k, alpha=alpha, has_next=has_next),
        grid_spec=grid_spec, out_shape=out_shape, compiler_params=_cparams(1), name="moe_combine",
    )(dest, *args)
    return (out[0], out[1]) if has_next else (out[0], None)


def kernel(x_prompt, x_sample, cache_k, cache_v, state_conv, page_table, c_prompt, c_sample, rel_bias_table,
           att_w_qkv, att_w_o, lambda_q1, lambda_k1, lambda_q2, lambda_k2, subln_g, conv_w_in, conv_w, conv_w_out,
           ada_w, ada_b, ln_g, ln_b, moe_w_group, moe_b_group, moe_w_expert, moe_b_expert, moe_w1, moe_w3, moe_w2):
    n_batch, seq, d = x_prompt.shape
    n_dec, t_new, _ = x_sample.shape
    depth = ada_w.shape[0]
    n_att = att_w_qkv.shape[0]
    d_att = N_HEADS * HEAD_DIM
    alpha = (2.0 * depth) ** 0.25
    mp, ms = n_batch * seq, n_dec * t_new
    n_all = mp + ms
    tm_p, tm_s = ROW_TILE, ms
    assert mp % ms == 0 and seq % tm_p == 0 and cache_k.shape[3] == N_HEADS and cache_k.shape[4] == HEAD_DIM

    n_seq = n_batch + n_dec
    pad = -n_seq % 8
    c_all = jnp.concatenate([c_prompt, c_sample, jnp.zeros((pad, d), F32)], axis=0)
    mod_all = _adaln(c_all, ada_w, ada_b).reshape(depth, n_seq + pad, 6, d)
    mod_p = _Mod(mod_all[:, :n_batch], seq, tm_p, per_row=False)
    mod_s_rows = jnp.repeat(jnp.swapaxes(mod_all[:, n_batch:n_seq], 1, 2), t_new, axis=2)
    mod_s = _Mod(mod_s_rows.reshape(depth * 6, ms, d), t_new, tm_s, per_row=True)

    xp = x_prompt.reshape(mp, d)
    xs = x_sample.reshape(ms, d)
    hp = _modulate(xp, mod_p, 0, tm_p)
    hs = _modulate(xs, mod_s, 0, tm_s)

    tiles = _prompt_bias_tiles(rel_bias_table, ATT_TILE)
    zero_state = jnp.zeros((n_batch, CONV_WIDTH - 1, d), F32)
    kp_buf = vp_buf = True
    ks_rows, vs_rows, conv_p, conv_s = [], [], [], []

    for l in range(depth):
        if l % 2 == 0:
            a = l // 2
            lam_init = 0.8 - 0.6 * math.exp(-0.3 * l)
            lam = (jnp.exp(jnp.sum(lambda_q1[a] * lambda_k1[a])) - jnp.exp(jnp.sum(lambda_q2[a] * lambda_k2[a]))
                   + lam_init).astype(F32)
            w_qkv = att_w_qkv[a].astype(BF16)
            wq, wk, wv = w_qkv[:, :d_att], w_qkv[:, d_att:2 * d_att], w_qkv[:, 2 * d_att:]
            (q_p,) = _project(hp, wq, tm=1024, tn=1024, out_dtypes=[BF16])
            kp_buf, k_p = _project(hp, wk, tm=1024, tn=1024, out_dtypes=[F32, BF16], stacked=kp_buf, layer=a,
                                   n_stack=n_att)
            vp_buf, v_p = _project(hp, wv, tm=1024, tn=1024, out_dtypes=[F32, BF16], stacked=vp_buf, layer=a,
                                   n_stack=n_att)
            o_p = _prompt_attention(q_p, k_p, v_p, tiles, lam, subln_g[a], lam_init, n_batch, seq)
            (qkv_s,) = _project(hs, w_qkv, tm=ms, tn=1024, out_dtypes=[F32])
            q_s, k_s, v_s = qkv_s[:, :d_att], qkv_s[:, d_att:2 * d_att], qkv_s[:, 2 * d_att:]
            ks_rows.append(k_s.reshape(n_dec, t_new, N_HEADS, HEAD_DIM))
            vs_rows.append(v_s.reshape(n_dec, t_new, N_HEADS, HEAD_DIM))
            o_s = _sample_attention(q_s, k_s, v_s, cache_k, cache_v, page_table, a, rel_bias_table, lam,
                                    subln_g[a], lam_init)
            w_out = att_w_o[a].astype(BF16)
        else:
            ci = l // 2
            w_in = conv_w_in[ci].astype(BF16)
            o_p, st_p = _conv_in(hp, w_in, conv_w[ci], zero_state, rows_per_seq=seq, tm=512, tn=512)
            o_s, st_s = _conv_in_short(hs, w_in, conv_w[ci], state_conv[ci], t=t_new, tn=512)
            conv_p.append(st_p)
            conv_s.append(st_s)
            w_out = conv_w_out[ci].astype(BF16)

        w_router = jnp.zeros((d, LANES), F32).at[:, :N_GROUPS].set(moe_w_group[l]) \
            .at[:, N_GROUPS:N_GROUPS + N_EXPERTS].set(moe_w_expert[l])
        b_router = jnp.zeros((1, LANES), F32).at[0, :N_GROUPS].set(moe_b_group[l]) \
            .at[0, N_GROUPS:N_GROUPS + N_EXPERTS].set(moe_b_expert[l])
        xp, h2_all, ids_p, wts_p = _post_mixer(o_p, w_out, xp, mod_p, l, ln_g[l, 0], ln_b[l, 0], w_router, b_router,
                                               tm=tm_p, alpha=alpha, h2_all=None, row0=0, n_all=n_all)
        xs, h2_all, ids_s, wts_s = _post_mixer(o_s, w_out, xs, mod_s, l, ln_g[l, 0], ln_b[l, 0], w_router, b_router,
                                               tm=tm_s, alpha=alpha, h2_all=h2_all, row0=mp, n_all=n_all)
        ids = jnp.concatenate([ids_p[:, :TOP_EXPERT], ids_s[:, :TOP_EXPERT]], axis=0)
        dest, block_expert, n_used = _moe_dispatch(ids, MOE_ROWS)
        y_slots = _moe_experts(h2_all, dest, block_expert, n_used, moe_w1, moe_w3, moe_w2, l, TOP_EXPERT)
        has_next = l + 1 < depth
        xp, hp = _combine(xp, y_slots, dest[:TOP_EXPERT * mp], wts_p, mod_p, l, ln_g[l, 1], ln_b[l, 1], tm=tm_p,
                          alpha=alpha, has_next=has_next, top_k=TOP_EXPERT)
        xs, hs = _combine(xs, y_slots, dest[TOP_EXPERT * mp:], wts_s, mod_s, l, ln_g[l, 1], ln_b[l, 1], tm=tm_s,
                          alpha=alpha, has_next=has_next, top_k=TOP_EXPERT)

    kv_shape = (n_att, n_batch, seq, N_HEADS, HEAD_DIM)
    return (xp.reshape(n_batch, seq, d), xs.reshape(n_dec, t_new, d),
            kp_buf.reshape(kv_shape), vp_buf.reshape(kv_shape), jnp.stack(conv_p),
            jnp.stack(ks_rows), jnp.stack(vs_rows), jnp.stack(conv_s))
```

```python
import functools
import math

import jax
import jax.numpy as jnp
from jax import lax
from jax.experimental import pallas as pl
from jax.experimental.pallas import tpu as pltpu

F32, BF16, I32 = jnp.float32, jnp.bfloat16, jnp.int32
HIGHEST = lax.Precision.HIGHEST

HEAD_DIM = 128
HALF_DIM = HEAD_DIM // 2
N_HEADS_A = 8
N_HEADS_B = 8
N_HEADS = N_HEADS_A + N_HEADS_B
MOBA_BLOCK = 256
MOBA_TOPK = 3
REL_BUCKETS = 32
REL_MAX_DIST = 128
N_GROUPS = 4
EXPERTS_PER_GROUP = 8
N_EXPERTS = N_GROUPS * EXPERTS_PER_GROUP
TOP_EXPERT = 2
CONV_WIDTH = 3
LN_EPS = 1e-5
NEG_INF = -1e30

LANES = 128
ATT_TILE = MOBA_BLOCK
ROW_TILE = 256
MOE_ROWS = 256
PAGES_PER_STEP = 4
VMEM_LIMIT = 56 << 20


def _cparams(n_axes, vmem=VMEM_LIMIT):
    return pltpu.CompilerParams(dimension_semantics=("arbitrary",) * n_axes, vmem_limit_bytes=vmem)


def _smem_spec():
    return pl.BlockSpec(memory_space=pltpu.SMEM)


def _layer_norm(y, g, b):
    mu = jnp.mean(y, axis=-1, keepdims=True)
    yc = y - mu
    var = jnp.mean(yc * yc, axis=-1, keepdims=True)
    return yc * lax.rsqrt(var + LN_EPS) * g + b


def _silu(x):
    return x * (1.0 / (1.0 + jnp.exp(-x)))


def _dot_nt(a, b):
    return lax.dot_general(a, b, (((1,), (1,)), ((), ())), preferred_element_type=F32)


def _adaln_kernel(c_ref, w_ref, b_ref, o_ref):
    s = _silu(c_ref[...]).astype(BF16)
    o_ref[...] = jnp.dot(s, w_ref[...].astype(BF16), preferred_element_type=F32) + b_ref[...]


def _adaln(c_all, ada_w, ada_b):
    depth, d, n = ada_w.shape
    rows = c_all.shape[0]
    tn = 1024
    return pl.pallas_call(
        _adaln_kernel,
        grid=(depth, n // tn),
        in_specs=[pl.BlockSpec((rows, d), lambda l, j: (0, 0)),
                  pl.BlockSpec((None, d, tn), lambda l, j: (l, 0, j)),
                  pl.BlockSpec((None, 1, tn), lambda l, j: (l, 0, j))],
        out_specs=pl.BlockSpec((None, rows, tn), lambda l, j: (l, 0, j)),
        out_shape=jax.ShapeDtypeStruct((depth, rows, n), F32),
        compiler_params=_cparams(2),
        name="adaln",
    )(c_all, ada_w, ada_b.reshape(depth, 1, n))


class _Mod:
    def __init__(self, table, rows_per_seq, row_tile, per_row):
        self.table, self.rows_per_seq, self.row_tile, self.per_row = table, rows_per_seq, row_tile, per_row

    def spec(self, layer, which, tn=None, row_arg=0, col_arg=None):
        d = self.table.shape[-1]
        tn = d if tn is None else tn
        tiles_per_seq = max(self.rows_per_seq // self.row_tile, 1)
        n_seq = self.table.shape[1] if not self.per_row else None

        def index(*g):
            col = 0 if col_arg is None else g[col_arg]
            if self.per_row:
                return (layer * 6 + which, g[row_arg], col)
            return ((layer * n_seq + g[row_arg] // tiles_per_seq) * 6 + which, 0, col)

        rows = self.row_tile if self.per_row else 1
        return pl.BlockSpec((None, rows, tn), index)

    @property
    def array(self):
        if self.per_row:
            return self.table
        depth, n_seq, six, d = self.table.shape
        return self.table.reshape(depth * n_seq * six, 1, d)


def _vec_spec(d):
    return pl.BlockSpec((1, d), lambda *g: (0, 0))


def _modulate_kernel(x_ref, shift_ref, scale_ref, h_ref):
    h_ref[...] = (x_ref[...] * (1.0 + scale_ref[...]) + shift_ref[...]).astype(h_ref.dtype)


def _modulate(x, mod, layer, tm):
    m, d = x.shape
    return pl.pallas_call(
        _modulate_kernel,
        grid=(m // tm,),
        in_specs=[pl.BlockSpec((tm, d), lambda i: (i, 0)), mod.spec(layer, 0), mod.spec(layer, 1)],
        out_specs=pl.BlockSpec((tm, d), lambda i: (i, 0)),
        out_shape=jax.ShapeDtypeStruct((m, d), BF16),
        compiler_params=_cparams(1),
        name="modulate",
    )(x, mod.array, mod.array)


def _matmul_kernel(x_ref, w_ref, *refs, n_alias):
    acc = jnp.dot(x_ref[...], w_ref[...], preferred_element_type=F32)
    for o_ref in refs[n_alias:]:
        o_ref[...] = acc.astype(o_ref.dtype)


def _project(x, w, *, tm, tn, out_dtypes, stacked=None, layer=0, n_stack=1):
    m, k = x.shape
    n = w.shape[1]
    in_specs = [pl.BlockSpec((tm, k), lambda i, j: (i, 0)), pl.BlockSpec((k, tn), lambda i, j: (0, j))]
    args = [x, w]
    out_specs, out_shape, aliases = [], [], {}
    n_alias = 0
    for dt in out_dtypes:
        if dt == F32 and stacked is not None:
            out_specs.append(pl.BlockSpec((None, tm, tn), lambda i, j: (layer, i, j)))
            out_shape.append(jax.ShapeDtypeStruct((n_stack, m, n), F32))
            if stacked is not True:
                in_specs.append(pl.BlockSpec(memory_space=pl.ANY))
                args.append(stacked)
                aliases[len(args) - 1] = len(out_shape) - 1
                n_alias = 1
        else:
            out_specs.append(pl.BlockSpec((tm, tn), lambda i, j: (i, j)))
            out_shape.append(jax.ShapeDtypeStruct((m, n), dt))
    return pl.pallas_call(
        functools.partial(_matmul_kernel, n_alias=n_alias),
        grid=(m // tm, n // tn),
        in_specs=in_specs, out_specs=out_specs, out_shape=out_shape,
        input_output_aliases=aliases,
        compiler_params=_cparams(2),
        name="project",
    )(*args)


def _t5_bucket(n):
    n = jnp.maximum(n, 0)
    max_exact = REL_BUCKETS // 2
    nf = jnp.maximum(n, 1).astype(F32)
    large = max_exact + (jnp.log(nf / max_exact) / math.log(REL_MAX_DIST / max_exact)
                         * (REL_BUCKETS - max_exact)).astype(I32)
    return jnp.where(n < max_exact, n, jnp.minimum(large, REL_BUCKETS - 1))


def _bias_of_distance(table, dist):
    bucket = _t5_bucket(dist)
    out = jnp.zeros((table.shape[1],) + bucket.shape, F32)
    for b in range(REL_BUCKETS):
        out = out + jnp.where(bucket == b, table[b].reshape((-1,) + (1,) * bucket.ndim), 0.0)
    return out


def _prompt_bias_tiles(table, t):
    assert t >= REL_MAX_DIST
    kk = jnp.arange(t)[:, None]
    qq = jnp.arange(t)[None, :]
    d0 = qq - kk
    far = table[REL_BUCKETS - 1][:, None, None]
    same = jnp.where(d0 >= 0, (_bias_of_distance(table, d0) - far) * LOG2E, NEG_INF)
    prev = (_bias_of_distance(table, d0 + t) - far) * LOG2E
    return jnp.stack([same, prev], axis=1).astype(F32)


LOG2E = 1.4426950408889634
V_ROWS = HEAD_DIM + 16


def _transpose_v(v_ref, vt_ref, t):
    ones = jnp.ones((V_ROWS - HEAD_DIM, t), BF16)
    for c in range(v_ref.shape[0] // t):
        vt_ref[c, 0:HEAD_DIM, :] = v_ref[c * t:(c + 1) * t, :].astype(F32).T.astype(BF16)
        vt_ref[c, HEAD_DIM:V_ROWS, :] = ones


def _attend(ts, vts, m_ref, acc_ref, idx, masks=None):
    m_old = m_ref[idx]
    tops = [jnp.max(tt, axis=0, keepdims=True) for tt in ts]
    if masks is not None:
        tops = [jnp.where(mk > 0.0, tp, NEG_INF) for tp, mk in zip(tops, masks)]
    m_new = functools.reduce(jnp.maximum, tops, m_old)
    acc = jnp.exp2(m_old - m_new) * acc_ref[idx]
    for i, (tt, vt) in enumerate(zip(ts, vts)):
        m_use = m_new if masks is None else jnp.where(masks[i] > 0.0, m_new, -NEG_INF)
        acc = acc + jnp.dot(vt, jnp.exp2(tt - m_use).astype(BF16), preferred_element_type=F32)
    acc_ref[idx] = acc
    m_ref[idx] = m_new


def _sweep_past_blocks(n_past, step):
    def pair(jj, c):
        step([2 * jj, 2 * jj + 1])
        return c

    lax.fori_loop(0, n_past // 2, pair, 0)

    @pl.when(n_past % 2 == 1)
    def _():
        step([n_past - 1])


def _init_stats(m_ref, l_ref, acc_ref):
    m_ref[...] = jnp.full(m_ref.shape, NEG_INF, F32)
    l_ref[...] = jnp.zeros(l_ref.shape, F32)
    acc_ref[...] = jnp.zeros(acc_ref.shape, F32)


def _key_block(k_ref, j, t):
    return k_ref[pl.ds(pl.multiple_of(j * t, t), t), :]


def _diff_attn_kernel(lam_ref, q_ref, k_ref, v_ref, bias_ref, g_ref, o_ref, vt_ref, m_ref, acc_ref, *, t, out_scale):
    qi = pl.program_id(2)

    @pl.when(qi == 0)
    def _():
        _transpose_v(v_ref, vt_ref, t)

    q = q_ref[...]
    lane = lax.broadcasted_iota(I32, q.shape, 1)
    qs = (jnp.where(lane < HALF_DIM, q, jnp.zeros_like(q)), jnp.where(lane >= HALF_DIM, q, jnp.zeros_like(q)))
    scale = (HALF_DIM ** -0.5) * LOG2E
    m_ref[...] = jnp.full(m_ref.shape, NEG_INF, F32)
    acc_ref[...] = jnp.zeros(acc_ref.shape, F32)

    def step(blocks, bias=None):
        for idx in range(2):
            ts = [_dot_nt(_key_block(k_ref, j, t), qs[idx]) * scale for j in blocks]
            if bias is not None:
                ts = [tt + bias for tt in ts]
            _attend(ts, [vt_ref[j] for j in blocks], m_ref, acc_ref, idx)

    step([qi], bias_ref[0])

    @pl.when(qi >= 1)
    def _():
        step([qi - 1], bias_ref[1])

    _sweep_past_blocks(jnp.maximum(qi - 1, 0), step)

    o = (acc_ref[0, 0:HEAD_DIM] / acc_ref[0, HEAD_DIM:HEAD_DIM + 1]
         - lam_ref[0] * (acc_ref[1, 0:HEAD_DIM] / acc_ref[1, HEAD_DIM:HEAD_DIM + 1]))
    ms = jnp.mean(o * o, axis=0, keepdims=True)
    o = o * lax.rsqrt(ms + LN_EPS) * g_ref[...] * out_scale
    o_ref[...] = o.T.astype(o_ref.dtype)


def _moba_attn_kernel(q_ref, k_ref, v_ref, kf_ref, bias_ref, o_in_ref, o_ref,
                      vt_ref, kmean_ref, sel_ref, m_ref, acc_ref, *, t):
    del o_in_ref
    qi = pl.program_id(2)
    nb = k_ref.shape[0] // t

    @pl.when(qi == 0)
    def _():
        _transpose_v(v_ref, vt_ref, t)
        for c in range(nb):
            kmean_ref[c:c + 1, :] = jnp.mean(kf_ref[c * t:(c + 1) * t, :], axis=0, keepdims=True)

    q = q_ref[...]
    gate = _dot_nt(kmean_ref[...].astype(BF16), q)
    blk = lax.broadcasted_iota(I32, gate.shape, 0)
    rank = jnp.zeros(gate.shape, F32)
    for n in range(nb):
        row = gate[n:n + 1, :]
        beats = jnp.logical_or(row > gate, jnp.logical_and(row == gate, n < blk))
        rank = rank + jnp.where(jnp.logical_and(beats, n < qi), 1.0, 0.0)
    sel_ref[...] = jnp.where(jnp.logical_and(blk < qi, rank < MOBA_TOPK), 1.0, 0.0)

    scale = (HEAD_DIM ** -0.5) * LOG2E
    m_ref[...] = jnp.full(m_ref.shape, NEG_INF, F32)
    acc_ref[...] = jnp.zeros(acc_ref.shape, F32)

    def step(blocks, bias=None, masked=True):
        ts = [_dot_nt(_key_block(k_ref, j, t), q) * scale for j in blocks]
        if bias is not None:
            ts = [tt + bias for tt in ts]
        masks = [sel_ref[pl.ds(j, 1), :] for j in blocks] if masked else None
        _attend(ts, [vt_ref[j] for j in blocks], m_ref, acc_ref, 0, masks)

    step([qi], bias_ref[0], masked=False)

    @pl.when(qi >= 1)
    def _():
        step([qi - 1], bias_ref[1])

    _sweep_past_blocks(jnp.maximum(qi - 1, 0), step)
    o_ref[...] = (acc_ref[0, 0:HEAD_DIM] / acc_ref[0, HEAD_DIM:HEAD_DIM + 1]).T.astype(o_ref.dtype)


def _prompt_attention(q, k, v, k_f32, layer, tiles, lam, subln_g, lam_init, n_batch, seq):
    m, d_att = q.shape
    t = ATT_TILE
    assert seq % t == 0 and t == MOBA_BLOCK
    nq = seq // t
    nb = seq // t
    na = N_HEADS_A

    def specs(h0):
        return [pl.BlockSpec((t, HEAD_DIM), lambda b, h, i: (b * nq + i, h0 + h)),
                pl.BlockSpec((seq, HEAD_DIM), lambda b, h, i: (b, h0 + h)),
                pl.BlockSpec((seq, HEAD_DIM), lambda b, h, i: (b, h0 + h))]

    def tile_spec(h0):
        return pl.BlockSpec((None, 2, t, t), lambda b, h, i: (h0 + h, 0, 0, 0))

    stats = [pltpu.VMEM((2, 1, t), F32), pltpu.VMEM((2, V_ROWS, t), F32)]
    o = pl.pallas_call(
        functools.partial(_diff_attn_kernel, t=t, out_scale=1.0 - lam_init),
        grid=(n_batch, na, nq),
        in_specs=[_smem_spec()] + specs(0) + [tile_spec(0), pl.BlockSpec((HEAD_DIM, 1), lambda b, h, i: (0, 0))],
        out_specs=pl.BlockSpec((t, HEAD_DIM), lambda b, h, i: (b * nq + i, h)),
        out_shape=jax.ShapeDtypeStruct((m, d_att), BF16),
        scratch_shapes=[pltpu.VMEM((nb, V_ROWS, t), BF16)] + stats,
        compiler_params=_cparams(3),
        name="diff_attention",
    )(lam.reshape(1), q, k, v, tiles, subln_g.reshape(HEAD_DIM, 1))
    return pl.pallas_call(
        functools.partial(_moba_attn_kernel, t=t),
        grid=(n_batch, N_HEADS_B, nq),
        in_specs=specs(na) + [pl.BlockSpec((None, seq, HEAD_DIM), lambda b, h, i: (layer, b, na + h)),
                              tile_spec(na), pl.BlockSpec(memory_space=pl.ANY)],
        out_specs=pl.BlockSpec((t, HEAD_DIM), lambda b, h, i: (b * nq + i, na + h)),
        out_shape=jax.ShapeDtypeStruct((m, d_att), BF16),
        scratch_shapes=[pltpu.VMEM((nb, V_ROWS, t), BF16), pltpu.VMEM((nb, HEAD_DIM), F32),
                        pltpu.VMEM((nb, t), F32)] + stats,
        input_output_aliases={5: 0},
        compiler_params=_cparams(3),
        name="moba_attention",
    )(q, k, v, k_f32, tiles, o)


def _decode_kernel(pt_ref, lam_ref, q_ref, knew_ref, vnew_ref, far_ref, dlast_ref, bnew_ref, g_ref, *refs,
                   n_pages_step, n_steps, n_blocks, out_scale):
    del pt_ref
    p_step = n_pages_step
    k_pages = refs[:p_step]
    v_pages = refs[p_step:2 * p_step]
    o_ref = refs[2 * p_step]
    (qrows_ref, s_ref, p_ref, pv_ref, kpad_ref, vpad_ref, m_ref, l_ref, acc_ref,
     gate_ref, pm_ref, pl_ref, po_ref) = refs[2 * p_step + 1:]
    step = pl.program_id(1)
    t_new = q_ref.shape[0]
    hr = 2 * t_new
    half = N_HEADS_A * hr
    page = LANES
    pages_per_blk = MOBA_BLOCK // page
    blks_per_step = p_step // pages_per_blk
    na = N_HEADS_A

    def rows(h):
        return slice(h * hr, (h + 1) * hr)

    def cols(h):
        return slice(h * HEAD_DIM, (h + 1) * HEAD_DIM)

    def page_head(ref, h):
        return ref[pl.ds(h, page, stride=N_HEADS), :].astype(BF16)

    @pl.when(step == 0)
    def _():
        for h in range(N_HEADS):
            qh = q_ref[:, cols(h)]
            if h < na:
                qh = qh * (HALF_DIM ** -0.5)
                lane = lax.broadcasted_iota(I32, qh.shape, 1)
                blk = jnp.concatenate([jnp.where(lane < HALF_DIM, qh, 0.0), jnp.where(lane >= HALF_DIM, qh, 0.0)],
                                      axis=0)
            else:
                blk = jnp.concatenate([qh * (HEAD_DIM ** -0.5), jnp.zeros_like(qh)], axis=0)
            qrows_ref[rows(h), :] = blk.astype(BF16)
        _init_stats(m_ref, l_ref, acc_ref)
        gate_ref[...] = jnp.zeros(gate_ref.shape, F32)
        pm_ref[...] = jnp.zeros(pm_ref.shape, F32)
        pl_ref[...] = jnp.zeros(pl_ref.shape, F32)

    def diff_softmax(s, width):
        m_old = m_ref[...]
        m_new = jnp.maximum(m_old, jnp.max(s, axis=1, keepdims=True))
        a = jnp.exp(m_old - m_new)
        p = jnp.exp(s - m_new)
        l_ref[...] = a * l_ref[...] + jnp.sum(p, axis=1, keepdims=True)
        m_ref[...] = m_new
        p_ref[0:half, 0:width] = p.astype(BF16)
        return a

    for h in range(N_HEADS):
        qh = qrows_ref[rows(h), :]
        for u in range(p_step):
            s_ref[rows(h), u * page:(u + 1) * page] = _dot_nt(qh, page_head(k_pages[u], h))

    last_f = jnp.where(step == n_steps - 1, 1.0, 0.0)
    bias = far_ref[...] + dlast_ref[...] * last_f
    a = diff_softmax(s_ref[0:half, :] + bias[0:half], p_step * page)
    raw = s_ref[half:, :]
    sb = raw + bias[half:]
    blk_lane = lax.broadcasted_iota(I32, pm_ref.shape, 1)
    for c in range(blks_per_step):
        kc = slice(c * MOBA_BLOCK, (c + 1) * MOBA_BLOCK)
        hit = blk_lane == step * blks_per_step + c
        m_c = jnp.max(sb[:, kc], axis=1, keepdims=True)
        p_c = jnp.exp(sb[:, kc] - m_c)
        gate_ref[...] = jnp.where(hit, jnp.sum(raw[:, kc], axis=1, keepdims=True), gate_ref[...])
        pm_ref[...] = jnp.where(hit, m_c, pm_ref[...])
        pl_ref[...] = jnp.where(hit, jnp.sum(p_c, axis=1, keepdims=True), pl_ref[...])
        p_ref[half:, kc] = p_c.astype(BF16)

    def pv(h, us):
        ph = p_ref[rows(h), :]
        return sum(jnp.dot(ph[:, u * page:(u + 1) * page], page_head(v_pages[u], h), preferred_element_type=F32)
                   for u in us)

    for h in range(na):
        pv_ref[rows(h), :] = pv(h, range(p_step))
    for hb in range(N_HEADS_B):
        for c in range(blks_per_step):
            po_ref[step * blks_per_step + c, rows(hb), :] = pv(na + hb, range(c * pages_per_blk,
                                                                                (c + 1) * pages_per_blk))
    acc_ref[...] = a * acc_ref[...] + pv_ref[0:half, :]

    @pl.when(step == n_steps - 1)
    def _():
        kpad_ref[...] = jnp.zeros(kpad_ref.shape, F32)
        vpad_ref[...] = jnp.zeros(vpad_ref.shape, F32)
        kpad_ref[0:t_new, :] = knew_ref[...]
        vpad_ref[0:t_new, :] = vnew_ref[...]
        for h in range(N_HEADS):
            s_ref[rows(h), 0:page] = _dot_nt(qrows_ref[rows(h), :], kpad_ref[:, cols(h)].astype(BF16))
        sn = s_ref[:, 0:page] + bnew_ref[...]
        a_new = diff_softmax(sn[0:half], page)
        m_own = jnp.max(sn[half:], axis=1, keepdims=True)
        p_own = jnp.exp(sn[half:] - m_own)
        l_own = jnp.sum(p_own, axis=1, keepdims=True)
        p_ref[half:, 0:page] = p_own.astype(BF16)
        for h in range(N_HEADS):
            pv_ref[rows(h), :] = jnp.dot(p_ref[rows(h), 0:page], vpad_ref[:, cols(h)].astype(BF16),
                                         preferred_element_type=F32)
        o = (a_new * acc_ref[...] + pv_ref[0:half, :]) / l_ref[...]
        lam = lam_ref[0]
        for h in range(na):
            od = o[h * hr:h * hr + t_new] - lam * o[h * hr + t_new:(h + 1) * hr]
            ms = jnp.mean(od * od, axis=1, keepdims=True)
            o_ref[:, cols(h)] = od * lax.rsqrt(ms + LN_EPS) * g_ref[...] * out_scale

        lane_f = blk_lane.astype(F32)
        gate = jnp.where(blk_lane < n_blocks, gate_ref[...], -jnp.inf)
        sel = jnp.zeros(gate.shape, jnp.bool_)
        for _ in range(min(MOBA_TOPK, n_blocks)):
            best = jnp.max(gate, axis=1, keepdims=True)
            first = jnp.min(jnp.where(gate == best, lane_f, float(LANES)), axis=1, keepdims=True)
            pick = lane_f == first
            sel = jnp.logical_or(sel, pick)
            gate = jnp.where(pick, -jnp.inf, gate)
        pm = pm_ref[...]
        m_all = jnp.maximum(m_own, jnp.max(jnp.where(sel, pm, NEG_INF), axis=1, keepdims=True))
        w = jnp.where(sel, jnp.exp(jnp.where(sel, pm, NEG_INF) - m_all), 0.0)
        w_own = jnp.exp(m_own - m_all)
        l_all = w_own * l_own + jnp.sum(w * pl_ref[...], axis=1, keepdims=True)
        o_all = w_own * pv_ref[half:, :]
        for n in range(n_blocks):
            o_all = o_all + w[:, n:n + 1] * po_ref[n]
        o_all = o_all / l_all
        for hb in range(N_HEADS_B):
            o_ref[:, cols(na + hb)] = o_all[hb * hr:hb * hr + t_new]


def _sample_attention(q, k_new, v_new, cache_k, cache_v, page_table, layer, table, lam, subln_g, lam_init):
    n_seq, n_pages = page_table.shape
    t_new = q.shape[0] // n_seq
    d_att = q.shape[1]
    page = cache_k.shape[2]
    past = n_pages * page
    p_step = PAGES_PER_STEP
    assert page == LANES and n_pages % p_step == 0 and past % MOBA_BLOCK == 0 and p_step % (MOBA_BLOCK // page) == 0
    assert page >= REL_MAX_DIST and t_new == 8 and past // MOBA_BLOCK <= LANES
    n_steps = n_pages // p_step
    n_blocks = past // MOBA_BLOCK
    hr = 2 * t_new
    n_rows = N_HEADS * hr
    half = N_HEADS_A * hr
    ck = cache_k.reshape(cache_k.shape[0], cache_k.shape[1], page * N_HEADS, HEAD_DIM)
    cv = cache_v.reshape(cache_v.shape[0], cache_v.shape[1], page * N_HEADS, HEAD_DIM)

    tt = jnp.arange(t_new)[:, None]
    rr = jnp.arange(page)[None, :]
    far = table[REL_BUCKETS - 1]
    dlast = _bias_of_distance(table, page + tt - rr) - far[:, None, None]
    dnew = tt - rr
    bnew = jnp.where(jnp.logical_and(dnew >= 0, rr < t_new), _bias_of_distance(table, dnew), NEG_INF)
    is_diff = (jnp.arange(N_HEADS) < N_HEADS_A)[:, None, None]
    dlast = jnp.concatenate([dlast, jnp.where(is_diff, dlast, 0.0)], axis=1).reshape(n_rows, page)
    bnew = jnp.concatenate([bnew, jnp.where(is_diff, bnew, 0.0)], axis=1).reshape(n_rows, page)
    dlast = jnp.concatenate([jnp.zeros((n_rows, (p_step - 1) * page), F32), dlast.astype(F32)], axis=1)
    far_rows = jnp.repeat(far.astype(F32), hr).reshape(n_rows, 1)

    def page_spec(u):
        return pl.BlockSpec((None, None, page * N_HEADS, HEAD_DIM),
                            lambda b, s, pt: (layer, pt[b * n_pages + s * p_step + u], 0, 0))

    def full(shape):
        return pl.BlockSpec(shape, lambda b, s, pt: (0,) * len(shape))

    row_spec = pl.BlockSpec((t_new, d_att), lambda b, s, pt: (b, 0))
    grid_spec = pltpu.PrefetchScalarGridSpec(
        num_scalar_prefetch=1,
        grid=(n_seq, n_steps),
        in_specs=[_smem_spec(), row_spec, row_spec, row_spec, full((n_rows, 1)), full((n_rows, p_step * page)),
                  full((n_rows, page)), full((1, HEAD_DIM))]
                 + [page_spec(u) for u in range(p_step)] * 2,
        out_specs=row_spec,
        scratch_shapes=[pltpu.VMEM((n_rows, HEAD_DIM), BF16), pltpu.VMEM((n_rows, p_step * page), F32),
                        pltpu.VMEM((n_rows, p_step * page), BF16), pltpu.VMEM((n_rows, HEAD_DIM), F32),
                        pltpu.VMEM((page, d_att), F32), pltpu.VMEM((page, d_att), F32),
                        pltpu.VMEM((half, 1), F32), pltpu.VMEM((half, 1), F32), pltpu.VMEM((half, HEAD_DIM), F32),
                        pltpu.VMEM((n_rows - half, LANES), F32), pltpu.VMEM((n_rows - half, LANES), F32),
                        pltpu.VMEM((n_rows - half, LANES), F32),
                        pltpu.VMEM((n_blocks, n_rows - half, HEAD_DIM), F32)],
    )
    return pl.pallas_call(
        functools.partial(_decode_kernel, n_pages_step=p_step, n_steps=n_steps, n_blocks=n_blocks,
                          out_scale=1.0 - lam_init),
        grid_spec=grid_spec,
        out_shape=jax.ShapeDtypeStruct(q.shape, F32),
        compiler_params=_cparams(2),
        name="decode_attention",
    )(page_table.reshape(-1), lam.reshape(1), q, k_new, v_new, far_rows, dlast, bnew.astype(F32),
      subln_g.reshape(1, HEAD_DIM), *([ck] * p_step), *([cv] * p_step))


def _conv_in_kernel(x_ref, wb_ref, wc_ref, wv_ref, cw_ref, st_ref, z_ref, tail_ref, carry_ref, *, tiles_per_seq):
    i = pl.program_id(1)
    x = x_ref[...]
    gb = jnp.dot(x, wb_ref[...], preferred_element_type=F32)
    u = jnp.dot(x, wc_ref[...], preferred_element_type=F32) * jnp.dot(x, wv_ref[...], preferred_element_type=F32)
    tm = u.shape[0]

    @pl.when(i % tiles_per_seq == 0)
    def _():
        carry_ref[...] = st_ref[...]

    prev = carry_ref[...]
    row = lax.broadcasted_iota(I32, u.shape, 0)
    u1 = jnp.where(row == 0, prev[1:2], pltpu.roll(u, 1, axis=0))
    u2 = jnp.where(row == 0, prev[0:1], jnp.where(row == 1, prev[1:2], pltpu.roll(u, 2, axis=0)))
    cw = cw_ref[...]
    z_ref[...] = (gb * (cw[0:1] * u2 + cw[1:2] * u1 + cw[2:3] * u)).astype(z_ref.dtype)
    tail = u[tm - (CONV_WIDTH - 1):tm]
    carry_ref[...] = tail
    tail_ref[...] = tail


def _conv_in(h, w_in, conv_w, state, *, rows_per_seq, tm, tn):
    m, d = h.shape
    n_seq = m // rows_per_seq
    assert rows_per_seq % tm == 0 and tm >= CONV_WIDTH - 1
    tiles_per_seq = rows_per_seq // tm
    nj = d // tn
    st_spec = pl.BlockSpec((None, CONV_WIDTH - 1, tn), lambda j, i: (i // tiles_per_seq, 0, j))
    return pl.pallas_call(
        functools.partial(_conv_in_kernel, tiles_per_seq=tiles_per_seq),
        grid=(nj, m // tm),
        in_specs=[pl.BlockSpec((tm, d), lambda j, i: (i, 0))] + _w_in_specs(d, tn, nj, lambda j, i: j)
                 + [pl.BlockSpec((CONV_WIDTH, tn), lambda j, i: (0, j)), st_spec],
        out_specs=[pl.BlockSpec((tm, tn), lambda j, i: (i, j)), st_spec],
        out_shape=[jax.ShapeDtypeStruct((m, d), BF16), jax.ShapeDtypeStruct((n_seq, CONV_WIDTH - 1, d), F32)],
        scratch_shapes=[pltpu.VMEM((CONV_WIDTH - 1, tn), F32)],
        compiler_params=_cparams(2),
        name="conv_in",
    )(h, w_in, w_in, w_in, conv_w, state)


def _w_in_specs(d, tn, nj, col_of):
    def spec(part):
        return pl.BlockSpec((d, tn), lambda *g: (0, part * nj + col_of(*g)))
    return [spec(part) for part in range(3)]


def _conv_in_short_kernel(x_ref, wb_ref, wc_ref, wv_ref, cw_ref, p0_ref, p1_ref, z_ref, u_ref, *, t):
    x = x_ref[...]
    gb = jnp.dot(x, wb_ref[...], preferred_element_type=F32)
    u = jnp.dot(x, wc_ref[...], preferred_element_type=F32) * jnp.dot(x, wv_ref[...], preferred_element_type=F32)
    pos = lax.rem(lax.broadcasted_iota(I32, u.shape, 0), t)
    u1 = jnp.where(pos == 0, p1_ref[...], pltpu.roll(u, 1, axis=0))
    u2 = jnp.where(pos == 0, p0_ref[...], jnp.where(pos == 1, p1_ref[...], pltpu.roll(u, 2, axis=0)))
    cw = cw_ref[...]
    z_ref[...] = (gb * (cw[0:1] * u2 + cw[1:2] * u1 + cw[2:3] * u)).astype(z_ref.dtype)
    u_ref[...] = u


def _conv_in_short(h, w_in, conv_w, state, *, t, tn):
    m, d = h.shape
    n_seq = m // t
    assert t >= CONV_WIDTH - 1
    nj = d // tn
    p0 = jnp.repeat(state[:, 0, :], t, axis=0)
    p1 = jnp.repeat(state[:, 1, :], t, axis=0)
    col = pl.BlockSpec((m, tn), lambda j: (0, j))
    z, u = pl.pallas_call(
        functools.partial(_conv_in_short_kernel, t=t),
        grid=(nj,),
        in_specs=[pl.BlockSpec((m, d), lambda j: (0, 0))] + _w_in_specs(d, tn, nj, lambda j: j)
                 + [pl.BlockSpec((CONV_WIDTH, tn), lambda j: (0, j)), col, col],
        out_specs=[col, col],
        out_shape=[jax.ShapeDtypeStruct((m, d), BF16), jax.ShapeDtypeStruct((m, d), F32)],
        compiler_params=_cparams(1),
        name="conv_in_short",
    )(h, w_in, w_in, w_in, conv_w, p0, p1)
    return z, u.reshape(n_seq, t, d)[:, t - (CONV_WIDTH - 1):]


def _route(logits):
    lane = lax.broadcasted_iota(I32, logits.shape, 1).astype(F32)
    g_mask = lane < N_GROUPS
    g_exp = jnp.exp(jnp.where(g_mask, logits, -jnp.inf)
                    - jnp.max(jnp.where(g_mask, logits, -jnp.inf), axis=1, keepdims=True))
    g_prob = g_exp / jnp.sum(g_exp, axis=1, keepdims=True)
    g_p = jnp.max(g_prob, axis=1, keepdims=True)
    g_idx = jnp.min(jnp.where(jnp.logical_and(g_prob == g_p, g_mask), lane, float(LANES)), axis=1, keepdims=True)
    lo = N_GROUPS + EXPERTS_PER_GROUP * g_idx
    e_mask = jnp.logical_and(lane >= lo, lane < lo + EXPERTS_PER_GROUP)
    e_logit = jnp.where(e_mask, logits, -jnp.inf)
    e_exp = jnp.exp(e_logit - jnp.max(e_logit, axis=1, keepdims=True))
    e_prob = jnp.where(e_mask, e_exp / jnp.sum(e_exp, axis=1, keepdims=True), -1.0)
    p1 = jnp.max(e_prob, axis=1, keepdims=True)
    i1 = jnp.min(jnp.where(e_prob == p1, lane, float(LANES)), axis=1, keepdims=True)
    rest = jnp.where(lane == i1, -1.0, e_prob)
    p2 = jnp.max(rest, axis=1, keepdims=True)
    i2 = jnp.min(jnp.where(rest == p2, lane, float(LANES)), axis=1, keepdims=True)
    tot = p1 + p2
    ids = jnp.where(lane == 0.0, i1 - N_GROUPS, jnp.where(lane == 1.0, i2 - N_GROUPS, 0.0)).astype(I32)
    wts = jnp.where(lane == 0.0, g_p * (p1 / tot), jnp.where(lane == 1.0, g_p * (p2 / tot), 0.0))
    return ids, wts


def _store_token_major(ref, x):
    rows, width = x.shape
    s = width // LANES
    for c in range(s):
        ref[pl.ds(c, rows, stride=s), :] = x[:, c * LANES:(c + 1) * LANES]


def _load_token_major(ref, rows, s):
    return jnp.concatenate([ref[pl.ds(c, rows, stride=s), :] for c in range(s)], axis=1)


def _token_copy(src, tok, dst, slot, s, sem):
    return pltpu.make_async_copy(src.at[pl.ds(pl.multiple_of(tok * s, s), s), :],
                                 dst.at[pl.ds(pl.multiple_of(slot * s, s), s), :], sem)


def _post_mixer_kernel(o_ref, w_ref, x_ref, gate_ref, lng_ref, lnb_ref, shift_ref, scale_ref, wr_ref, br_ref,
                       *refs, alpha, n_alias):
    x1_ref, h2_ref, ids_ref, wts_ref = refs[n_alias:]
    f = jnp.dot(o_ref[...].astype(BF16), w_ref[...], preferred_element_type=F32)
    x1 = _layer_norm(alpha * x_ref[...] + (1.0 + gate_ref[...]) * f, lng_ref[...], lnb_ref[...])
    x1_ref[...] = x1
    h2 = x1 * (1.0 + scale_ref[...]) + shift_ref[...]
    _store_token_major(h2_ref, h2)
    logits = jnp.dot(h2.astype(BF16), wr_ref[...], preferred_element_type=F32) + br_ref[...]
    ids_ref[...], wts_ref[...] = _route(logits)


def _post_mixer(o, w, x, mod, layer, ln_g, ln_b, w_router, b_router, *, tm, alpha, h2_all, row0, n_all):
    m, d = x.shape
    k = o.shape[1]
    assert row0 % tm == 0
    blk0 = row0 // tm
    row = lambda i: (i, 0)
    in_specs = [pl.BlockSpec((tm, k), row), pl.BlockSpec((k, d), lambda i: (0, 0)), pl.BlockSpec((tm, d), row),
                mod.spec(layer, 2), _vec_spec(d), _vec_spec(d), mod.spec(layer, 3), mod.spec(layer, 4),
                pl.BlockSpec((d, LANES), lambda i: (0, 0)), _vec_spec(LANES)]
    args = [o, w, x, mod.array, ln_g.reshape(1, d), ln_b.reshape(1, d), mod.array, mod.array, w_router, b_router]
    aliases, n_alias = {}, 0
    if h2_all is not None:
        in_specs.append(pl.BlockSpec(memory_space=pl.ANY))
        args.append(h2_all)
        aliases, n_alias = {len(args) - 1: 1}, 1
    return pl.pallas_call(
        functools.partial(_post_mixer_kernel, alpha=alpha, n_alias=n_alias),
        grid=(m // tm,),
        in_specs=in_specs,
        out_specs=[pl.BlockSpec((tm, d), row), pl.BlockSpec((tm * (d // LANES), LANES), lambda i: (blk0 + i, 0)),
                   pl.BlockSpec((tm, LANES), row), pl.BlockSpec((tm, LANES), row)],
        out_shape=[jax.ShapeDtypeStruct((m, d), F32), jax.ShapeDtypeStruct((n_all * (d // LANES), LANES), F32),
                   jax.ShapeDtypeStruct((m, LANES), I32), jax.ShapeDtypeStruct((m, LANES), F32)],
        input_output_aliases=aliases,
        compiler_params=_cparams(1),
        name="post_mixer",
    )(*args)


def _moe_kernel(be_ref, dest_ref, nused_ref, h_hbm, w1_ref, w3_ref, w2_ref, y_ref,
                slot_ref, xbuf, w1b, w3b, w2b, sem, *, rows, chunks, top_k):
    i = pl.program_id(0)
    n_used = nused_ref[0]

    def gather(block, buf):
        base = block * rows

        def issue(r, c):
            _token_copy(h_hbm, lax.div(slot_ref[base + r], top_k), xbuf.at[buf], r, chunks, sem.at[buf]).start()
            return c

        lax.fori_loop(0, rows, issue, 0, unroll=8)

    @pl.when(i == 0)
    def _():
        def clear(s, c):
            slot_ref[s] = 0
            return c

        lax.fori_loop(0, slot_ref.shape[0], clear, 0, unroll=8)

        def place(a, c):
            slot_ref[dest_ref[a]] = a
            return c

        lax.fori_loop(0, dest_ref.shape[0], place, 0, unroll=8)
        gather(0, 0)

    @pl.when(i < n_used)
    def _():
        buf = lax.rem(i, 2)

        @pl.when(jnp.logical_or(i == 0, be_ref[i] != be_ref[jnp.maximum(i - 1, 0)]))
        def _():
            w1b[...] = w1_ref[...].astype(BF16)
            w3b[...] = w3_ref[...].astype(BF16)
            w2b[...] = w2_ref[...].astype(BF16)

        pltpu.make_async_copy(h_hbm.at[pl.ds(0, rows * chunks), :], xbuf.at[buf], sem.at[buf]).wait()
        x = _load_token_major(xbuf.at[buf], rows, chunks).astype(BF16)
        a = jnp.dot(x, w1b[...], preferred_element_type=F32)
        b = jnp.dot(x, w3b[...], preferred_element_type=F32)

        @pl.when(i + 1 < n_used)
        def _():
            gather(i + 1, 1 - buf)

        y = jnp.dot((_silu(a) * b).astype(BF16), w2b[...], preferred_element_type=F32)
        _store_token_major(y_ref, y)

    @pl.when(i >= n_used)
    def _():
        y_ref[...] = jnp.zeros(y_ref.shape, F32)


def _moe_dispatch(ids, rows):
    n_tok, k = ids.shape
    nk = n_tok * k
    flat_e = ids.reshape(nk)
    onehot = (flat_e[:, None] == jnp.arange(N_EXPERTS, dtype=I32)[None, :]).astype(I32)
    csum = jnp.cumsum(onehot, axis=0)
    counts = csum[-1]
    pos = jnp.sum((csum - onehot) * onehot, axis=1)
    padded = (counts + rows - 1) // rows * rows
    pad_end = jnp.cumsum(padded)
    pad_start = jnp.sum(jnp.where(onehot > 0, (pad_end - padded)[None, :], 0), axis=1)
    dest = (pad_start + pos).astype(I32)
    n_blocks = (nk + rows - 1) // rows + N_EXPERTS
    starts = jnp.arange(n_blocks, dtype=I32) * rows
    block_expert = jnp.minimum(jnp.sum((pad_end[None, :] <= starts[:, None]).astype(I32), axis=1),
                               N_EXPERTS - 1).astype(I32)
    n_used = (pad_end[-1] // rows).astype(I32).reshape(1)
    return dest, block_expert, n_used


def _moe_experts(h_all, dest, block_expert, n_used, w1, w3, w2, layer, top_k):
    rows = MOE_ROWS
    n_blocks = block_expert.shape[0]
    d, d_e = w1.shape[-2:]
    chunks = d // LANES
    grid_spec = pltpu.PrefetchScalarGridSpec(
        num_scalar_prefetch=3,
        grid=(n_blocks,),
        in_specs=[pl.BlockSpec(memory_space=pl.ANY),
                  pl.BlockSpec((None, None, d, d_e), lambda i, be, dst, nu: (layer, be[i], 0, 0)),
                  pl.BlockSpec((None, None, d, d_e), lambda i, be, dst, nu: (layer, be[i], 0, 0)),
                  pl.BlockSpec((None, None, d_e, d), lambda i, be, dst, nu: (layer, be[i], 0, 0))],
        out_specs=pl.BlockSpec((rows * chunks, LANES), lambda i, be, dst, nu: (i, 0)),
        scratch_shapes=[pltpu.SMEM((n_blocks * rows,), I32), pltpu.VMEM((2, rows * chunks, LANES), F32),
                        pltpu.VMEM((d, d_e), BF16), pltpu.VMEM((d, d_e), BF16), pltpu.VMEM((d_e, d), BF16),
                        pltpu.SemaphoreType.DMA((2,))],
    )
    return pl.pallas_call(
        functools.partial(_moe_kernel, rows=rows, chunks=chunks, top_k=top_k),
        grid_spec=grid_spec,
        out_shape=jax.ShapeDtypeStruct((n_blocks * rows * chunks, LANES), F32),
        compiler_params=_cparams(1),
        name="moe_experts",
    )(block_expert, dest, n_used, h_all, w1, w3, w2)


def _combine_kernel(dest_ref, x_ref, y_hbm, wts_ref, gate_ref, lng_ref, lnb_ref, *refs, tm, chunks, top_k, alpha,
                    has_next):
    if has_next:
        shift_ref, scale_ref, x2_ref, h_ref, ybuf, sem = refs
    else:
        x2_ref, ybuf, sem = refs
    i = pl.program_id(0)

    def gather(tile, slot):
        base = tile * (top_k * tm)

        def issue(r, c):
            for k in range(top_k):
                _token_copy(y_hbm, dest_ref[base + top_k * r + k], ybuf.at[slot, k], r, chunks, sem.at[slot]).start()
            return c

        lax.fori_loop(0, tm, issue, 0, unroll=4)

    @pl.when(i == 0)
    def _():
        gather(0, 0)

    slot = lax.rem(i, 2)

    @pl.when(i + 1 < pl.num_programs(0))
    def _():
        gather(i + 1, 1 - slot)

    for k in range(top_k):
        pltpu.make_async_copy(y_hbm.at[pl.ds(0, tm * chunks), :], ybuf.at[slot, k], sem.at[slot]).wait()
    wts = wts_ref[...]
    g = sum(wts[:, k:k + 1] * _load_token_major(ybuf.at[slot, k], tm, chunks) for k in range(top_k))
    x2 = _layer_norm(alpha * x_ref[...] + (1.0 + gate_ref[...]) * g, lng_ref[...], lnb_ref[...])
    x2_ref[...] = x2
    if has_next:
        h_ref[...] = (x2 * (1.0 + scale_ref[...]) + shift_ref[...]).astype(h_ref.dtype)


def _combine(x1, y_slots, dest, wts, mod, layer, ln_g, ln_b, *, tm, alpha, has_next, top_k):
    m, d = x1.shape
    chunks = d // LANES
    row = lambda i, dst: (i, 0)
    vec = pl.BlockSpec((1, d), lambda i, dst: (0, 0))
    in_specs = [pl.BlockSpec((tm, d), row), pl.BlockSpec(memory_space=pl.ANY), pl.BlockSpec((tm, LANES), row),
                mod.spec(layer, 5), vec, vec]
    args = [x1, y_slots, wts, mod.array, ln_g.reshape(1, d), ln_b.reshape(1, d)]
    out_specs = [pl.BlockSpec((tm, d), row)]
    out_shape = [jax.ShapeDtypeStruct((m, d), F32)]
    if has_next:
        in_specs += [mod.spec(layer + 1, 0), mod.spec(layer + 1, 1)]
        args += [mod.array, mod.array]
        out_specs.append(pl.BlockSpec((tm, d), row))
        out_shape.append(jax.ShapeDtypeStruct((m, d), BF16))
    grid_spec = pltpu.PrefetchScalarGridSpec(
        num_scalar_prefetch=1, grid=(m // tm,), in_specs=in_specs, out_specs=out_specs,
        scratch_shapes=[pltpu.VMEM((2, top_k, tm * chunks, LANES), F32), pltpu.SemaphoreType.DMA((2,))])
    out = pl.pallas_call(
        functools.partial(_combine_kernel, tm=tm, chunks=chunks, top_k=top_k, alpha=alpha, has_next=has_next),
        grid_spec=grid_spec, out_shape=out_shape, compiler_params=_cparams(1), name="moe_combine",
    )(dest, *args)
    return (out[0], out[1]) if has_next else (out[0], None)


def kernel(x_prompt, x_sample, cache_k, cache_v, state_conv, page_table, c_prompt, c_sample, rel_bias_table,
           att_w_qkv, att_w_o, lambda_q1, lambda_k1, lambda_q2, lambda_k2, subln_g, conv_w_in, conv_w, conv_w_out,
           ada_w, ada_b, ln_g, ln_b, moe_w_group, moe_b_group, moe_w_expert, moe_b_expert, moe_w1, moe_w3, moe_w2):
    n_batch, seq, d = x_prompt.shape
    n_dec, t_new, _ = x_sample.shape
    depth = ada_w.shape[0]
    n_att = att_w_qkv.shape[0]
    d_att = N_HEADS * HEAD_DIM
    alpha = (2.0 * depth) ** 0.25
    mp, ms = n_batch * seq, n_dec * t_new
    n_all = mp + ms
    tm_p, tm_s = ROW_TILE, ms
    assert mp % ms == 0 and seq % tm_p == 0 and cache_k.shape[3] == N_HEADS and cache_k.shape[4] == HEAD_DIM

    n_seq = n_batch + n_dec
    pad = -n_seq % 8
    c_all = jnp.concatenate([c_prompt, c_sample, jnp.zeros((pad, d), F32)], axis=0)
    mod_all = _adaln(c_all, ada_w, ada_b).reshape(depth, n_seq + pad, 6, d)
    mod_p = _Mod(mod_all[:, :n_batch], seq, tm_p, per_row=False)
    mod_s_rows = jnp.repeat(jnp.swapaxes(mod_all[:, n_batch:n_seq], 1, 2), t_new, axis=2)
    mod_s = _Mod(mod_s_rows.reshape(depth * 6, ms, d), t_new, tm_s, per_row=True)

    xp = x_prompt.reshape(mp, d)
    xs = x_sample.reshape(ms, d)
    hp = _modulate(xp, mod_p, 0, tm_p)
    hs = _modulate(xs, mod_s, 0, tm_s)

    tiles = _prompt_bias_tiles(rel_bias_table, ATT_TILE)
    zero_state = jnp.zeros((n_batch, CONV_WIDTH - 1, d), F32)
    kp_buf = vp_buf = True
    ks_rows, vs_rows, conv_p, conv_s = [], [], [], []

    for l in range(depth):
        if l % 2 == 0:
            a = l // 2
            lam_init = 0.8 - 0.6 * math.exp(-0.3 * l)
            lam = (jnp.exp(jnp.sum(lambda_q1[a] * lambda_k1[a])) - jnp.exp(jnp.sum(lambda_q2[a] * lambda_k2[a]))
                   + lam_init).astype(F32)
            w_qkv = att_w_qkv[a].astype(BF16)
            wq, wk, wv = w_qkv[:, :d_att], w_qkv[:, d_att:2 * d_att], w_qkv[:, 2 * d_att:]
            (q_p,) = _project(hp, wq, tm=1024, tn=1024, out_dtypes=[BF16])
            kp_buf, k_p = _project(hp, wk, tm=1024, tn=1024, out_dtypes=[F32, BF16], stacked=kp_buf, layer=a,
                                   n_stack=n_att)
            vp_buf, v_p = _project(hp, wv, tm=1024, tn=1024, out_dtypes=[F32, BF16], stacked=vp_buf, layer=a,
                                   n_stack=n_att)
            o_p = _prompt_attention(q_p, k_p, v_p, kp_buf, a, tiles, lam, subln_g[a], lam_init, n_batch, seq)
            (qkv_s,) = _project(hs, w_qkv, tm=ms, tn=1024, out_dtypes=[F32])
            q_s, k_s, v_s = qkv_s[:, :d_att], qkv_s[:, d_att:2 * d_att], qkv_s[:, 2 * d_att:]
            ks_rows.append(k_s.reshape(n_dec, t_new, N_HEADS, HEAD_DIM))
            vs_rows.append(v_s.reshape(n_dec, t_new, N_HEADS, HEAD_DIM))
            o_s = _sample_attention(q_s, k_s, v_s, cache_k, cache_v, page_table, a, rel_bias_table, lam,
                                    subln_g[a], lam_init)
            w_out = att_w_o[a].astype(BF16)
        else:
            ci = l // 2
            w_in = conv_w_in[ci].astype(BF16)
            o_p, st_p = _conv_in(hp, w_in, conv_w[ci], zero_state, rows_per_seq=seq, tm=512, tn=512)
            o_s, st_s = _conv_in_short(hs, w_in, conv_w[ci], state_conv[ci], t=t_new, tn=512)
            conv_p.append(st_p)
            conv_s.append(st_s)
            w_out = conv_w_out[ci].astype(BF16)

        w_router = jnp.zeros((d, LANES), F32).at[:, :N_GROUPS].set(moe_w_group[l]) \
            .at[:, N_GROUPS:N_GROUPS + N_EXPERTS].set(moe_w_expert[l]).astype(BF16)
        b_router = jnp.zeros((1, LANES), F32).at[0, :N_GROUPS].set(moe_b_group[l]) \
            .at[0, N_GROUPS:N_GROUPS + N_EXPERTS].set(moe_b_expert[l])
        xp, h2_all, ids_p, wts_p = _post_mixer(o_p, w_out, xp, mod_p, l, ln_g[l, 0], ln_b[l, 0], w_router, b_router,
                                               tm=tm_p, alpha=alpha, h2_all=None, row0=0, n_all=n_all)
        xs, h2_all, ids_s, wts_s = _post_mixer(o_s, w_out, xs, mod_s, l, ln_g[l, 0], ln_b[l, 0], w_router, b_router,
                                               tm=tm_s, alpha=alpha, h2_all=h2_all, row0=mp, n_all=n_all)
        ids = jnp.concatenate([ids_p[:, :TOP_EXPERT], ids_s[:, :TOP_EXPERT]], axis=0)
        dest, block_expert, n_used = _moe_dispatch(ids, MOE_ROWS)
        y_slots = _moe_experts(h2_all, dest, block_expert, n_used, moe_w1, moe_w3, moe_w2, l, TOP_EXPERT)
        has_next = l + 1 < depth
        xp, hp = _combine(xp, y_slots, dest[:TOP_EXPERT * mp], wts_p, mod_p, l, ln_g[l, 1], ln_b[l, 1], tm=tm_p,
                          alpha=alpha, has_next=has_next, top_k=TOP_EXPERT)
        xs, hs = _combine(xs, y_slots, dest[TOP_EXPERT * mp:], wts_s, mod_s, l, ln_g[l, 1], ln_b[l, 1], tm=tm_s,
                          alpha=alpha, has_next=has_next, top_k=TOP_EXPERT)

    kv_shape = (n_att, n_batch, seq, N_HEADS, HEAD_DIM)
    return (xp.reshape(n_batch, seq, d), xs.reshape(n_dec, t_new, d),
            kp_buf.reshape(kv_shape), vp_buf.reshape(kv_shape), jnp.stack(conv_p),
            jnp.stack(ks_rows), jnp.stack(vs_rows), jnp.stack(conv_s))
```

```python
import functools
import math

import jax
import jax.numpy as jnp
from jax import lax
from jax.experimental import pallas as pl
from jax.experimental.pallas import tpu as pltpu

F32, BF16, I32 = jnp.float32, jnp.bfloat16, jnp.int32
HIGHEST = lax.Precision.HIGHEST

HEAD_DIM = 128
HALF_DIM = HEAD_DIM // 2
N_HEADS_A = 8
N_HEADS_B = 8
N_HEADS = N_HEADS_A + N_HEADS_B
MOBA_BLOCK = 256
MOBA_TOPK = 3
REL_BUCKETS = 32
REL_MAX_DIST = 128
N_GROUPS = 4
EXPERTS_PER_GROUP = 8
N_EXPERTS = N_GROUPS * EXPERTS_PER_GROUP
TOP_EXPERT = 2
CONV_WIDTH = 3
LN_EPS = 1e-5
NEG_INF = -1e30

LANES = 128
ATT_TILE = MOBA_BLOCK
ROW_TILE = 256
MOE_ROWS = 256
MOE_ROW_BUFFERS = 3
PAGES_PER_STEP = 4
VMEM_LIMIT = 56 << 20


def _cparams(n_axes, vmem=VMEM_LIMIT):
    return pltpu.CompilerParams(dimension_semantics=("arbitrary",) * n_axes, vmem_limit_bytes=vmem)


def _smem_spec():
    return pl.BlockSpec(memory_space=pltpu.SMEM)


def _layer_norm(y, g, b):
    mu = jnp.mean(y, axis=-1, keepdims=True)
    yc = y - mu
    var = jnp.mean(yc * yc, axis=-1, keepdims=True)
    return yc * lax.rsqrt(var + LN_EPS) * g + b


def _silu(x):
    return x * (1.0 / (1.0 + jnp.exp(-x)))


def _dot_nt(a, b):
    return lax.dot_general(a, b, (((1,), (1,)), ((), ())), preferred_element_type=F32)


def _adaln_kernel(c_ref, w_ref, b_ref, o_ref):
    s = _silu(c_ref[...]).astype(BF16)
    o_ref[...] = jnp.dot(s, w_ref[...].astype(BF16), preferred_element_type=F32) + b_ref[...]


def _adaln(c_all, ada_w, ada_b):
    depth, d, n = ada_w.shape
    rows = c_all.shape[0]
    tn = 1024
    return pl.pallas_call(
        _adaln_kernel,
        grid=(depth, n // tn),
        in_specs=[pl.BlockSpec((rows, d), lambda l, j: (0, 0)),
                  pl.BlockSpec((None, d, tn), lambda l, j: (l, 0, j)),
                  pl.BlockSpec((None, 1, tn), lambda l, j: (l, 0, j))],
        out_specs=pl.BlockSpec((None, rows, tn), lambda l, j: (l, 0, j)),
        out_shape=jax.ShapeDtypeStruct((depth, rows, n), F32),
        compiler_params=_cparams(2),
        name="adaln",
    )(c_all, ada_w, ada_b.reshape(depth, 1, n))


class _Mod:
    def __init__(self, table, rows_per_seq, row_tile, per_row):
        self.table, self.rows_per_seq, self.row_tile, self.per_row = table, rows_per_seq, row_tile, per_row

    def spec(self, layer, which, tn=None, row_arg=0, col_arg=None):
        d = self.table.shape[-1]
        tn = d if tn is None else tn
        tiles_per_seq = max(self.rows_per_seq // self.row_tile, 1)
        n_seq = self.table.shape[1] if not self.per_row else None

        def index(*g):
            col = 0 if col_arg is None else g[col_arg]
            if self.per_row:
                return (layer * 6 + which, g[row_arg], col)
            return ((layer * n_seq + g[row_arg] // tiles_per_seq) * 6 + which, 0, col)

        rows = self.row_tile if self.per_row else 1
        return pl.BlockSpec((None, rows, tn), index)

    @property
    def array(self):
        if self.per_row:
            return self.table
        depth, n_seq, six, d = self.table.shape
        return self.table.reshape(depth * n_seq * six, 1, d)


def _vec_spec(d):
    return pl.BlockSpec((1, d), lambda *g: (0, 0))


def _modulate_kernel(x_ref, shift_ref, scale_ref, h_ref):
    h_ref[...] = (x_ref[...] * (1.0 + scale_ref[...]) + shift_ref[...]).astype(h_ref.dtype)


def _modulate(x, mod, layer, tm):
    m, d = x.shape
    return pl.pallas_call(
        _modulate_kernel,
        grid=(m // tm,),
        in_specs=[pl.BlockSpec((tm, d), lambda i: (i, 0)), mod.spec(layer, 0), mod.spec(layer, 1)],
        out_specs=pl.BlockSpec((tm, d), lambda i: (i, 0)),
        out_shape=jax.ShapeDtypeStruct((m, d), BF16),
        compiler_params=_cparams(1),
        name="modulate",
    )(x, mod.array, mod.array)


def _matmul_kernel(x_ref, w_ref, *refs, n_alias):
    acc = jnp.dot(x_ref[...], w_ref[...], preferred_element_type=F32)
    for o_ref in refs[n_alias:]:
        o_ref[...] = acc.astype(o_ref.dtype)


def _project(x, w, *, tm, tn, out_dtypes, stacked=None, layer=0, n_stack=1):
    m, k = x.shape
    n = w.shape[1]
    in_specs = [pl.BlockSpec((tm, k), lambda i, j: (i, 0)), pl.BlockSpec((k, tn), lambda i, j: (0, j))]
    args = [x, w]
    out_specs, out_shape, aliases = [], [], {}
    n_alias = 0
    for dt in out_dtypes:
        if dt == F32 and stacked is not None:
            out_specs.append(pl.BlockSpec((None, tm, tn), lambda i, j: (layer, i, j)))
            out_shape.append(jax.ShapeDtypeStruct((n_stack, m, n), F32))
            if stacked is not True:
                in_specs.append(pl.BlockSpec(memory_space=pl.ANY))
                args.append(stacked)
                aliases[len(args) - 1] = len(out_shape) - 1
                n_alias = 1
        else:
            out_specs.append(pl.BlockSpec((tm, tn), lambda i, j: (i, j)))
            out_shape.append(jax.ShapeDtypeStruct((m, n), dt))
    return pl.pallas_call(
        functools.partial(_matmul_kernel, n_alias=n_alias),
        grid=(m // tm, n // tn),
        in_specs=in_specs, out_specs=out_specs, out_shape=out_shape,
        input_output_aliases=aliases,
        compiler_params=_cparams(2),
        name="project",
    )(*args)


def _t5_bucket(n):
    n = jnp.maximum(n, 0)
    max_exact = REL_BUCKETS // 2
    nf = jnp.maximum(n, 1).astype(F32)
    large = max_exact + (jnp.log(nf / max_exact) / math.log(REL_MAX_DIST / max_exact)
                         * (REL_BUCKETS - max_exact)).astype(I32)
    return jnp.where(n < max_exact, n, jnp.minimum(large, REL_BUCKETS - 1))


def _bias_of_distance(table, dist):
    bucket = _t5_bucket(dist)
    out = jnp.zeros((table.shape[1],) + bucket.shape, F32)
    for b in range(REL_BUCKETS):
        out = out + jnp.where(bucket == b, table[b].reshape((-1,) + (1,) * bucket.ndim), 0.0)
    return out


def _prompt_bias_tiles(table, t):
    assert t >= REL_MAX_DIST
    kk = jnp.arange(t)[:, None]
    qq = jnp.arange(t)[None, :]
    d0 = qq - kk
    far = table[REL_BUCKETS - 1][:, None, None]
    same = jnp.where(d0 >= 0, (_bias_of_distance(table, d0) - far) * LOG2E, NEG_INF)
    prev = (_bias_of_distance(table, d0 + t) - far) * LOG2E
    return jnp.stack([same, prev], axis=1).astype(F32)


LOG2E = 1.4426950408889634
V_ROWS = HEAD_DIM + 16


def _transpose_v(v_ref, vt_ref, t):
    ones = jnp.ones((V_ROWS - HEAD_DIM, t), BF16)
    for c in range(v_ref.shape[0] // t):
        vt_ref[c, 0:HEAD_DIM, :] = v_ref[c * t:(c + 1) * t, :].astype(F32).T.astype(BF16)
        vt_ref[c, HEAD_DIM:V_ROWS, :] = ones


def _attend(ts, vts, m_ref, acc_ref, idx, masks=None):
    m_old = m_ref[idx]
    masks = [None] * len(ts) if masks is None else masks
    tops = [jnp.max(tt, axis=0, keepdims=True) for tt in ts]
    tops = [tp if mk is None else jnp.where(mk > 0.0, tp, NEG_INF) for tp, mk in zip(tops, masks)]
    m_new = functools.reduce(jnp.maximum, tops, m_old)
    acc = jnp.exp2(m_old - m_new) * acc_ref[idx]
    for tt, vt, mk in zip(ts, vts, masks):
        m_use = m_new if mk is None else jnp.where(mk > 0.0, m_new, -NEG_INF)
        acc = acc + jnp.dot(vt, jnp.exp2(tt - m_use).astype(BF16), preferred_element_type=F32)
    acc_ref[idx] = acc
    m_ref[idx] = m_new


SWEEP = 4


def _sweep_key_blocks(qi, step, near_biases):
    n_past = jnp.maximum(qi - 1, 0)
    done = (n_past // SWEEP) * SWEEP
    rest = n_past - done

    @pl.when(qi == 0)
    def _():
        step([qi], near_biases[:1])

    for r in range(SWEEP):
        @pl.when(jnp.logical_and(qi >= 1, rest == r))
        def _():
            step([qi, qi - 1] + [done + u for u in range(r)], near_biases + [None] * r)

    def sweep(jj, c):
        step([SWEEP * jj + u for u in range(SWEEP)], [None] * SWEEP)
        return c

    lax.fori_loop(0, n_past // SWEEP, sweep, 0)


def _init_stats(m_ref, l_ref, acc_ref):
    m_ref[...] = jnp.full(m_ref.shape, NEG_INF, F32)
    l_ref[...] = jnp.zeros(l_ref.shape, F32)
    acc_ref[...] = jnp.zeros(acc_ref.shape, F32)


def _key_block(k_ref, j, t):
    return k_ref[pl.ds(pl.multiple_of(j * t, t), t), :]


def _diff_attn_kernel(lam_ref, q_ref, k_ref, v_ref, bias_ref, g_ref, o_ref, vt_ref, m_ref, acc_ref, *, t, out_scale):
    qi = pl.program_id(2)

    @pl.when(qi == 0)
    def _():
        _transpose_v(v_ref, vt_ref, t)

    q = q_ref[...]
    lane = lax.broadcasted_iota(I32, q.shape, 1)
    qs = (jnp.where(lane < HALF_DIM, q, jnp.zeros_like(q)), jnp.where(lane >= HALF_DIM, q, jnp.zeros_like(q)))
    scale = (HALF_DIM ** -0.5) * LOG2E
    m_ref[...] = jnp.full(m_ref.shape, NEG_INF, F32)
    acc_ref[...] = jnp.zeros(acc_ref.shape, F32)

    def step(blocks, biases):
        for idx in range(2):
            ts = [_dot_nt(_key_block(k_ref, j, t), qs[idx]) * scale for j in blocks]
            ts = [tt if bias is None else tt + bias for tt, bias in zip(ts, biases)]
            _attend(ts, [vt_ref[j] for j in blocks], m_ref, acc_ref, idx)

    _sweep_key_blocks(qi, step, [bias_ref[0], bias_ref[1]])

    o = (acc_ref[0, 0:HEAD_DIM] / acc_ref[0, HEAD_DIM:HEAD_DIM + 1]
         - lam_ref[0] * (acc_ref[1, 0:HEAD_DIM] / acc_ref[1, HEAD_DIM:HEAD_DIM + 1]))
    ms = jnp.mean(o * o, axis=0, keepdims=True)
    o = o * lax.rsqrt(ms + LN_EPS) * g_ref[...] * out_scale
    o_ref[...] = o.T.astype(o_ref.dtype)


def _moba_attn_kernel(q_ref, k_ref, v_ref, kf_ref, bias_ref, o_in_ref, o_ref,
                      vt_ref, kmean_ref, sel_ref, m_ref, acc_ref, *, t):
    del o_in_ref
    qi = pl.program_id(2)
    nb = k_ref.shape[0] // t

    @pl.when(qi == 0)
    def _():
        _transpose_v(v_ref, vt_ref, t)
        for c in range(nb):
            kmean_ref[c:c + 1, :] = jnp.mean(kf_ref[c * t:(c + 1) * t, :], axis=0, keepdims=True)

    q = q_ref[...]
    gate = _dot_nt(kmean_ref[...].astype(BF16), q)
    blk = lax.broadcasted_iota(I32, gate.shape, 0)
    rank = jnp.zeros(gate.shape, F32)
    for n in range(nb):
        row = gate[n:n + 1, :]
        beats = jnp.logical_or(row > gate, jnp.logical_and(row == gate, n < blk))
        rank = rank + jnp.where(jnp.logical_and(beats, n < qi), 1.0, 0.0)
    sel_ref[...] = jnp.where(jnp.logical_and(blk < qi, rank < MOBA_TOPK), 1.0, 0.0)

    scale = (HEAD_DIM ** -0.5) * LOG2E
    m_ref[...] = jnp.full(m_ref.shape, NEG_INF, F32)
    acc_ref[...] = jnp.zeros(acc_ref.shape, F32)

    def step(blocks, biases):
        ts = [_dot_nt(_key_block(k_ref, j, t), q) * scale for j in blocks]
        ts = [tt if bias is None else tt + bias for tt, bias in zip(ts, biases)]
        masks = [None if j is qi else sel_ref[pl.ds(j, 1), :] for j in blocks]
        _attend(ts, [vt_ref[j] for j in blocks], m_ref, acc_ref, 0, masks)

    _sweep_key_blocks(qi, step, [bias_ref[0], bias_ref[1]])
    o_ref[...] = (acc_ref[0, 0:HEAD_DIM] / acc_ref[0, HEAD_DIM:HEAD_DIM + 1]).T.astype(o_ref.dtype)


def _prompt_attention(q, k, v, k_f32, layer, tiles, lam, subln_g, lam_init, n_batch, seq):
    m, d_att = q.shape
    t = ATT_TILE
    assert seq % t == 0 and t == MOBA_BLOCK
    nq = seq // t
    nb = seq // t
    na = N_HEADS_A

    def specs(h0):
        return [pl.BlockSpec((t, HEAD_DIM), lambda b, h, i: (b * nq + i, h0 + h)),
                pl.BlockSpec((seq, HEAD_DIM), lambda b, h, i: (b, h0 + h)),
                pl.BlockSpec((seq, HEAD_DIM), lambda b, h, i: (b, h0 + h))]

    def tile_spec(h0):
        return pl.BlockSpec((None, 2, t, t), lambda b, h, i: (h0 + h, 0, 0, 0))

    stats = [pltpu.VMEM((2, 1, t), F32), pltpu.VMEM((2, V_ROWS, t), F32)]
    o = pl.pallas_call(
        functools.partial(_diff_attn_kernel, t=t, out_scale=1.0 - lam_init),
        grid=(n_batch, na, nq),
        in_specs=[_smem_spec()] + specs(0) + [tile_spec(0), pl.BlockSpec((HEAD_DIM, 1), lambda b, h, i: (0, 0))],
        out_specs=pl.BlockSpec((t, HEAD_DIM), lambda b, h, i: (b * nq + i, h)),
        out_shape=jax.ShapeDtypeStruct((m, d_att), BF16),
        scratch_shapes=[pltpu.VMEM((nb, V_ROWS, t), BF16)] + stats,
        compiler_params=_cparams(3),
        name="diff_attention",
    )(lam.reshape(1), q, k, v, tiles, subln_g.reshape(HEAD_DIM, 1))
    return pl.pallas_call(
        functools.partial(_moba_attn_kernel, t=t),
        grid=(n_batch, N_HEADS_B, nq),
        in_specs=specs(na) + [pl.BlockSpec((None, seq, HEAD_DIM), lambda b, h, i: (layer, b, na + h)),
                              tile_spec(na), pl.BlockSpec(memory_space=pl.ANY)],
        out_specs=pl.BlockSpec((t, HEAD_DIM), lambda b, h, i: (b * nq + i, na + h)),
        out_shape=jax.ShapeDtypeStruct((m, d_att), BF16),
        scratch_shapes=[pltpu.VMEM((nb, V_ROWS, t), BF16), pltpu.VMEM((nb, HEAD_DIM), F32),
                        pltpu.VMEM((nb, t), F32)] + stats,
        input_output_aliases={5: 0},
        compiler_params=_cparams(3),
        name="moba_attention",
    )(q, k, v, k_f32, tiles, o)


def _decode_kernel(pt_ref, lam_ref, q_ref, knew_ref, vnew_ref, far_ref, dlast_ref, bnew_ref, g_ref, *refs,
                   n_pages_step, n_steps, n_blocks, out_scale):
    del pt_ref
    p_step = n_pages_step
    k_pages = refs[:p_step]
    v_pages = refs[p_step:2 * p_step]
    o_ref = refs[2 * p_step]
    (qrows_ref, s_ref, p_ref, pv_ref, kpad_ref, vpad_ref, m_ref, l_ref, acc_ref,
     gate_ref, pm_ref, pl_ref, po_ref) = refs[2 * p_step + 1:]
    step = pl.program_id(1)
    t_new = q_ref.shape[0]
    hr = 2 * t_new
    half = N_HEADS_A * hr
    page = LANES
    pages_per_blk = MOBA_BLOCK // page
    blks_per_step = p_step // pages_per_blk
    na = N_HEADS_A

    def rows(h):
        return slice(h * hr, (h + 1) * hr)

    def cols(h):
        return slice(h * HEAD_DIM, (h + 1) * HEAD_DIM)

    def page_head(ref, h):
        return ref[pl.ds(h, page, stride=N_HEADS), :].astype(BF16)

    @pl.when(step == 0)
    def _():
        for h in range(N_HEADS):
            qh = q_ref[:, cols(h)]
            if h < na:
                qh = qh * (HALF_DIM ** -0.5)
                lane = lax.broadcasted_iota(I32, qh.shape, 1)
                blk = jnp.concatenate([jnp.where(lane < HALF_DIM, qh, 0.0), jnp.where(lane >= HALF_DIM, qh, 0.0)],
                                      axis=0)
            else:
                blk = jnp.concatenate([qh * (HEAD_DIM ** -0.5), jnp.zeros_like(qh)], axis=0)
            qrows_ref[rows(h), :] = blk.astype(BF16)
        _init_stats(m_ref, l_ref, acc_ref)
        gate_ref[...] = jnp.zeros(gate_ref.shape, F32)
        pm_ref[...] = jnp.zeros(pm_ref.shape, F32)
        pl_ref[...] = jnp.zeros(pl_ref.shape, F32)

    def diff_softmax(s, width):
        m_old = m_ref[...]
        m_new = jnp.maximum(m_old, jnp.max(s, axis=1, keepdims=True))
        a = jnp.exp(m_old - m_new)
        p = jnp.exp(s - m_new)
        l_ref[...] = a * l_ref[...] + jnp.sum(p, axis=1, keepdims=True)
        m_ref[...] = m_new
        p_ref[0:half, 0:width] = p.astype(BF16)
        return a

    for h in range(N_HEADS):
        qh = qrows_ref[rows(h), :]
        for u in range(p_step):
            s_ref[rows(h), u * page:(u + 1) * page] = _dot_nt(qh, page_head(k_pages[u], h))

    last_f = jnp.where(step == n_steps - 1, 1.0, 0.0)
    bias = far_ref[...] + dlast_ref[...] * last_f
    a = diff_softmax(s_ref[0:half, :] + bias[0:half], p_step * page)
    raw = s_ref[half:, :]
    sb = raw + bias[half:]
    blk_lane = lax.broadcasted_iota(I32, pm_ref.shape, 1)
    for c in range(blks_per_step):
        kc = slice(c * MOBA_BLOCK, (c + 1) * MOBA_BLOCK)
        hit = blk_lane == step * blks_per_step + c
        m_c = jnp.max(sb[:, kc], axis=1, keepdims=True)
        p_c = jnp.exp(sb[:, kc] - m_c)
        gate_ref[...] = jnp.where(hit, jnp.sum(raw[:, kc], axis=1, keepdims=True), gate_ref[...])
        pm_ref[...] = jnp.where(hit, m_c, pm_ref[...])
        pl_ref[...] = jnp.where(hit, jnp.sum(p_c, axis=1, keepdims=True), pl_ref[...])
        p_ref[half:, kc] = p_c.astype(BF16)

    def pv(h, us):
        ph = p_ref[rows(h), :]
        return sum(jnp.dot(ph[:, u * page:(u + 1) * page], page_head(v_pages[u], h), preferred_element_type=F32)
                   for u in us)

    for h in range(na):
        pv_ref[rows(h), :] = pv(h, range(p_step))
    for hb in range(N_HEADS_B):
        for c in range(blks_per_step):
            po_ref[step * blks_per_step + c, rows(hb), :] = pv(na + hb, range(c * pages_per_blk,
                                                                                (c + 1) * pages_per_blk))
    acc_ref[...] = a * acc_ref[...] + pv_ref[0:half, :]

    @pl.when(step == n_steps - 1)
    def _():
        kpad_ref[...] = jnp.zeros(kpad_ref.shape, F32)
        vpad_ref[...] = jnp.zeros(vpad_ref.shape, F32)
        kpad_ref[0:t_new, :] = knew_ref[...]
        vpad_ref[0:t_new, :] = vnew_ref[...]
        for h in range(N_HEADS):
            s_ref[rows(h), 0:page] = _dot_nt(qrows_ref[rows(h), :], kpad_ref[:, cols(h)].astype(BF16))
        sn = s_ref[:, 0:page] + bnew_ref[...]
        a_new = diff_softmax(sn[0:half], page)
        m_own = jnp.max(sn[half:], axis=1, keepdims=True)
        p_own = jnp.exp(sn[half:] - m_own)
        l_own = jnp.sum(p_own, axis=1, keepdims=True)
        p_ref[half:, 0:page] = p_own.astype(BF16)
        for h in range(N_HEADS):
            pv_ref[rows(h), :] = jnp.dot(p_ref[rows(h), 0:page], vpad_ref[:, cols(h)].astype(BF16),
                                         preferred_element_type=F32)
        o = (a_new * acc_ref[...] + pv_ref[0:half, :]) / l_ref[...]
        lam = lam_ref[0]
        for h in range(na):
            od = o[h * hr:h * hr + t_new] - lam * o[h * hr + t_new:(h + 1) * hr]
            ms = jnp.mean(od * od, axis=1, keepdims=True)
            o_ref[:, cols(h)] = od * lax.rsqrt(ms + LN_EPS) * g_ref[...] * out_scale

        lane_f = blk_lane.astype(F32)
        gate = jnp.where(blk_lane < n_blocks, gate_ref[...], -jnp.inf)
        sel = jnp.zeros(gate.shape, jnp.bool_)
        for _ in range(min(MOBA_TOPK, n_blocks)):
            best = jnp.max(gate, axis=1, keepdims=True)
            first = jnp.min(jnp.where(gate == best, lane_f, float(LANES)), axis=1, keepdims=True)
            pick = lane_f == first
            sel = jnp.logical_or(sel, pick)
            gate = jnp.where(pick, -jnp.inf, gate)
        pm = pm_ref[...]
        m_all = jnp.maximum(m_own, jnp.max(jnp.where(sel, pm, NEG_INF), axis=1, keepdims=True))
        w = jnp.where(sel, jnp.exp(jnp.where(sel, pm, NEG_INF) - m_all), 0.0)
        w_own = jnp.exp(m_own - m_all)
        l_all = w_own * l_own + jnp.sum(w * pl_ref[...], axis=1, keepdims=True)
        o_all = w_own * pv_ref[half:, :]
        for n in range(n_blocks):
            o_all = o_all + w[:, n:n + 1] * po_ref[n]
        o_all = o_all / l_all
        for hb in range(N_HEADS_B):
            o_ref[:, cols(na + hb)] = o_all[hb * hr:hb * hr + t_new]


def _sample_attention(q, k_new, v_new, cache_k, cache_v, page_table, layer, table, lam, subln_g, lam_init):
    n_seq, n_pages = page_table.shape
    t_new = q.shape[0] // n_seq
    d_att = q.shape[1]
    page = cache_k.shape[2]
    past = n_pages * page
    p_step = PAGES_PER_STEP
    assert page == LANES and n_pages % p_step == 0 and past % MOBA_BLOCK == 0 and p_step % (MOBA_BLOCK // page) == 0
    assert page >= REL_MAX_DIST and t_new == 8 and past // MOBA_BLOCK <= LANES
    n_steps = n_pages // p_step
    n_blocks = past // MOBA_BLOCK
    hr = 2 * t_new
    n_rows = N_HEADS * hr
    half = N_HEADS_A * hr
    ck = cache_k.reshape(cache_k.shape[0], cache_k.shape[1], page * N_HEADS, HEAD_DIM)
    cv = cache_v.reshape(cache_v.shape[0], cache_v.shape[1], page * N_HEADS, HEAD_DIM)

    tt = jnp.arange(t_new)[:, None]
    rr = jnp.arange(page)[None, :]
    far = table[REL_BUCKETS - 1]
    dlast = _bias_of_distance(table, page + tt - rr) - far[:, None, None]
    dnew = tt - rr
    bnew = jnp.where(jnp.logical_and(dnew >= 0, rr < t_new), _bias_of_distance(table, dnew), NEG_INF)
    is_diff = (jnp.arange(N_HEADS) < N_HEADS_A)[:, None, None]
    dlast = jnp.concatenate([dlast, jnp.where(is_diff, dlast, 0.0)], axis=1).reshape(n_rows, page)
    bnew = jnp.concatenate([bnew, jnp.where(is_diff, bnew, 0.0)], axis=1).reshape(n_rows, page)
    dlast = jnp.concatenate([jnp.zeros((n_rows, (p_step - 1) * page), F32), dlast.astype(F32)], axis=1)
    far_rows = jnp.repeat(far.astype(F32), hr).reshape(n_rows, 1)

    def page_spec(u):
        return pl.BlockSpec((None, None, page * N_HEADS, HEAD_DIM),
                            lambda b, s, pt: (layer, pt[b * n_pages + s * p_step + u], 0, 0))

    def full(shape):
        return pl.BlockSpec(shape, lambda b, s, pt: (0,) * len(shape))

    row_spec = pl.BlockSpec((t_new, d_att), lambda b, s, pt: (b, 0))
    grid_spec = pltpu.PrefetchScalarGridSpec(
        num_scalar_prefetch=1,
        grid=(n_seq, n_steps),
        in_specs=[_smem_spec(), row_spec, row_spec, row_spec, full((n_rows, 1)), full((n_rows, p_step * page)),
                  full((n_rows, page)), full((1, HEAD_DIM))]
                 + [page_spec(u) for u in range(p_step)] * 2,
        out_specs=row_spec,
        scratch_shapes=[pltpu.VMEM((n_rows, HEAD_DIM), BF16), pltpu.VMEM((n_rows, p_step * page), F32),
                        pltpu.VMEM((n_rows, p_step * page), BF16), pltpu.VMEM((n_rows, HEAD_DIM), F32),
                        pltpu.VMEM((page, d_att), F32), pltpu.VMEM((page, d_att), F32),
                        pltpu.VMEM((half, 1), F32), pltpu.VMEM((half, 1), F32), pltpu.VMEM((half, HEAD_DIM), F32),
                        pltpu.VMEM((n_rows - half, LANES), F32), pltpu.VMEM((n_rows - half, LANES), F32),
                        pltpu.VMEM((n_rows - half, LANES), F32),
                        pltpu.VMEM((n_blocks, n_rows - half, HEAD_DIM), F32)],
    )
    return pl.pallas_call(
        functools.partial(_decode_kernel, n_pages_step=p_step, n_steps=n_steps, n_blocks=n_blocks,
                          out_scale=1.0 - lam_init),
        grid_spec=grid_spec,
        out_shape=jax.ShapeDtypeStruct(q.shape, F32),
        compiler_params=_cparams(2),
        name="decode_attention",
    )(page_table.reshape(-1), lam.reshape(1), q, k_new, v_new, far_rows, dlast, bnew.astype(F32),
      subln_g.reshape(1, HEAD_DIM), *([ck] * p_step), *([cv] * p_step))


def _conv_in_kernel(x_ref, wb_ref, wc_ref, wv_ref, cw_ref, st_ref, z_ref, tail_ref, carry_ref, *, tiles_per_seq):
    i = pl.program_id(1)
    x = x_ref[...]
    gb = jnp.dot(x, wb_ref[...], preferred_element_type=F32)
    u = jnp.dot(x, wc_ref[...], preferred_element_type=F32) * jnp.dot(x, wv_ref[...], preferred_element_type=F32)
    tm = u.shape[0]

    @pl.when(i % tiles_per_seq == 0)
    def _():
        carry_ref[...] = st_ref[...]

    prev = carry_ref[...]
    row = lax.broadcasted_iota(I32, u.shape, 0)
    u1 = jnp.where(row == 0, prev[1:2], pltpu.roll(u, 1, axis=0))
    u2 = jnp.where(row == 0, prev[0:1], jnp.where(row == 1, prev[1:2], pltpu.roll(u, 2, axis=0)))
    cw = cw_ref[...]
    z_ref[...] = (gb * (cw[0:1] * u2 + cw[1:2] * u1 + cw[2:3] * u)).astype(z_ref.dtype)
    tail = u[tm - (CONV_WIDTH - 1):tm]
    carry_ref[...] = tail
    tail_ref[...] = tail


def _conv_in(h, w_in, conv_w, state, *, rows_per_seq, tm, tn):
    m, d = h.shape
    n_seq = m // rows_per_seq
    assert rows_per_seq % tm == 0 and tm >= CONV_WIDTH - 1
    tiles_per_seq = rows_per_seq // tm
    nj = d // tn
    st_spec = pl.BlockSpec((None, CONV_WIDTH - 1, tn), lambda j, i: (i // tiles_per_seq, 0, j))
    return pl.pallas_call(
        functools.partial(_conv_in_kernel, tiles_per_seq=tiles_per_seq),
        grid=(nj, m // tm),
        in_specs=[pl.BlockSpec((tm, d), lambda j, i: (i, 0))] + _w_in_specs(d, tn, nj, lambda j, i: j)
                 + [pl.BlockSpec((CONV_WIDTH, tn), lambda j, i: (0, j)), st_spec],
        out_specs=[pl.BlockSpec((tm, tn), lambda j, i: (i, j)), st_spec],
        out_shape=[jax.ShapeDtypeStruct((m, d), BF16), jax.ShapeDtypeStruct((n_seq, CONV_WIDTH - 1, d), F32)],
        scratch_shapes=[pltpu.VMEM((CONV_WIDTH - 1, tn), F32)],
        compiler_params=_cparams(2),
        name="conv_in",
    )(h, w_in, w_in, w_in, conv_w, state)


def _w_in_specs(d, tn, nj, col_of):
    def spec(part):
        return pl.BlockSpec((d, tn), lambda *g: (0, part * nj + col_of(*g)))
    return [spec(part) for part in range(3)]


def _conv_in_short_kernel(x_ref, wb_ref, wc_ref, wv_ref, cw_ref, p0_ref, p1_ref, z_ref, u_ref, *, t):
    x = x_ref[...]
    gb = jnp.dot(x, wb_ref[...], preferred_element_type=F32)
    u = jnp.dot(x, wc_ref[...], preferred_element_type=F32) * jnp.dot(x, wv_ref[...], preferred_element_type=F32)
    pos = lax.rem(lax.broadcasted_iota(I32, u.shape, 0), t)
    u1 = jnp.where(pos == 0, p1_ref[...], pltpu.roll(u, 1, axis=0))
    u2 = jnp.where(pos == 0, p0_ref[...], jnp.where(pos == 1, p1_ref[...], pltpu.roll(u, 2, axis=0)))
    cw = cw_ref[...]
    z_ref[...] = (gb * (cw[0:1] * u2 + cw[1:2] * u1 + cw[2:3] * u)).astype(z_ref.dtype)
    u_ref[...] = u


def _conv_in_short(h, w_in, conv_w, state, *, t, tn):
    m, d = h.shape
    n_seq = m // t
    assert t >= CONV_WIDTH - 1
    nj = d // tn
    p0 = jnp.repeat(state[:, 0, :], t, axis=0)
    p1 = jnp.repeat(state[:, 1, :], t, axis=0)
    col = pl.BlockSpec((m, tn), lambda j: (0, j))
    z, u = pl.pallas_call(
        functools.partial(_conv_in_short_kernel, t=t),
        grid=(nj,),
        in_specs=[pl.BlockSpec((m, d), lambda j: (0, 0))] + _w_in_specs(d, tn, nj, lambda j: j)
                 + [pl.BlockSpec((CONV_WIDTH, tn), lambda j: (0, j)), col, col],
        out_specs=[col, col],
        out_shape=[jax.ShapeDtypeStruct((m, d), BF16), jax.ShapeDtypeStruct((m, d), F32)],
        compiler_params=_cparams(1),
        name="conv_in_short",
    )(h, w_in, w_in, w_in, conv_w, p0, p1)
    return z, u.reshape(n_seq, t, d)[:, t - (CONV_WIDTH - 1):]


def _route(logits):
    lane = lax.broadcasted_iota(I32, logits.shape, 1).astype(F32)
    g_mask = lane < N_GROUPS
    g_exp = jnp.exp(jnp.where(g_mask, logits, -jnp.inf)
                    - jnp.max(jnp.where(g_mask, logits, -jnp.inf), axis=1, keepdims=True))
    g_prob = g_exp / jnp.sum(g_exp, axis=1, keepdims=True)
    g_p = jnp.max(g_prob, axis=1, keepdims=True)
    g_idx = jnp.min(jnp.where(jnp.logical_and(g_prob == g_p, g_mask), lane, float(LANES)), axis=1, keepdims=True)
    lo = N_GROUPS + EXPERTS_PER_GROUP * g_idx
    e_mask = jnp.logical_and(lane >= lo, lane < lo + EXPERTS_PER_GROUP)
    e_logit = jnp.where(e_mask, logits, -jnp.inf)
    e_exp = jnp.exp(e_logit - jnp.max(e_logit, axis=1, keepdims=True))
    e_prob = jnp.where(e_mask, e_exp / jnp.sum(e_exp, axis=1, keepdims=True), -1.0)
    p1 = jnp.max(e_prob, axis=1, keepdims=True)
    i1 = jnp.min(jnp.where(e_prob == p1, lane, float(LANES)), axis=1, keepdims=True)
    rest = jnp.where(lane == i1, -1.0, e_prob)
    p2 = jnp.max(rest, axis=1, keepdims=True)
    i2 = jnp.min(jnp.where(rest == p2, lane, float(LANES)), axis=1, keepdims=True)
    tot = p1 + p2
    ids = jnp.where(lane == 0.0, i1 - N_GROUPS, jnp.where(lane == 1.0, i2 - N_GROUPS, 0.0)).astype(I32)
    wts = jnp.where(lane == 0.0, g_p * (p1 / tot), jnp.where(lane == 1.0, g_p * (p2 / tot), 0.0))
    return ids, wts


def _store_token_major(ref, x):
    rows, width = x.shape
    s = width // LANES
    for c in range(s):
        ref[pl.ds(c, rows, stride=s), :] = x[:, c * LANES:(c + 1) * LANES]


def _load_token_major(ref, rows, s):
    return jnp.concatenate([ref[pl.ds(c, rows, stride=s), :] for c in range(s)], axis=1)


def _token_copy(src, tok, dst, slot, s, sem):
    return pltpu.make_async_copy(src.at[pl.ds(pl.multiple_of(tok * s, s), s), :],
                                 dst.at[pl.ds(pl.multiple_of(slot * s, s), s), :], sem)


def _post_mixer_kernel(o_ref, w_ref, x_ref, gate_ref, lng_ref, lnb_ref, shift_ref, scale_ref, wr_ref, br_ref,
                       *refs, alpha, n_alias):
    x1_ref, h2_ref, ids_ref, wts_ref = refs[n_alias:]
    f = jnp.dot(o_ref[...].astype(BF16), w_ref[...], preferred_element_type=F32)
    x1 = _layer_norm(alpha * x_ref[...] + (1.0 + gate_ref[...]) * f, lng_ref[...], lnb_ref[...])
    x1_ref[...] = x1
    h2 = x1 * (1.0 + scale_ref[...]) + shift_ref[...]
    _store_token_major(h2_ref, h2)
    logits = jnp.dot(h2.astype(BF16), wr_ref[...], preferred_element_type=F32) + br_ref[...]
    ids_ref[...], wts_ref[...] = _route(logits)


def _post_mixer(o, w, x, mod, layer, ln_g, ln_b, w_router, b_router, *, tm, alpha, h2_all, row0, n_all):
    m, d = x.shape
    k = o.shape[1]
    assert row0 % tm == 0
    blk0 = row0 // tm
    row = lambda i: (i, 0)
    in_specs = [pl.BlockSpec((tm, k), row), pl.BlockSpec((k, d), lambda i: (0, 0)), pl.BlockSpec((tm, d), row),
                mod.spec(layer, 2), _vec_spec(d), _vec_spec(d), mod.spec(layer, 3), mod.spec(layer, 4),
                pl.BlockSpec((d, LANES), lambda i: (0, 0)), _vec_spec(LANES)]
    args = [o, w, x, mod.array, ln_g.reshape(1, d), ln_b.reshape(1, d), mod.array, mod.array, w_router, b_router]
    aliases, n_alias = {}, 0
    if h2_all is not None:
        in_specs.append(pl.BlockSpec(memory_space=pl.ANY))
        args.append(h2_all)
        aliases, n_alias = {len(args) - 1: 1}, 1
    return pl.pallas_call(
        functools.partial(_post_mixer_kernel, alpha=alpha, n_alias=n_alias),
        grid=(m // tm,),
        in_specs=in_specs,
        out_specs=[pl.BlockSpec((tm, d), row), pl.BlockSpec((tm * (d // LANES), LANES), lambda i: (blk0 + i, 0)),
                   pl.BlockSpec((tm, LANES), row), pl.BlockSpec((tm, LANES), row)],
        out_shape=[jax.ShapeDtypeStruct((m, d), F32), jax.ShapeDtypeStruct((n_all * (d // LANES), LANES), F32),
                   jax.ShapeDtypeStruct((m, LANES), I32), jax.ShapeDtypeStruct((m, LANES), F32)],
        input_output_aliases=aliases,
        compiler_params=_cparams(1),
        name="post_mixer",
    )(*args)


def _moe_kernel(be_ref, dest_ref, nused_ref, h_hbm, w1_ref, w3_ref, w2_ref, y_ref,
                slot_ref, xbuf, w1b, w3b, w2b, sem, *, rows, chunks, top_k):
    i = pl.program_id(0)
    n_used = nused_ref[0]
    n_buf = xbuf.shape[0]

    def gather(block, buf):
        base = block * rows

        def issue(r, c):
            _token_copy(h_hbm, lax.div(slot_ref[base + r], top_k), xbuf.at[buf], r, chunks, sem.at[buf]).start()
            return c

        lax.fori_loop(0, rows, issue, 0, unroll=8)

    @pl.when(i == 0)
    def _():
        def clear(s, c):
            slot_ref[s] = 0
            return c

        lax.fori_loop(0, slot_ref.shape[0], clear, 0, unroll=8)

        def place(a, c):
            slot_ref[dest_ref[a]] = a
            return c

        lax.fori_loop(0, dest_ref.shape[0], place, 0, unroll=8)
        for ahead in range(n_buf - 1):
            @pl.when(ahead < n_used)
            def _():
                gather(ahead, ahead)

    @pl.when(i < n_used)
    def _():
        buf = lax.rem(i, n_buf)

        @pl.when(i + (n_buf - 1) < n_used)
        def _():
            gather(i + (n_buf - 1), lax.rem(i + (n_buf - 1), n_buf))

        @pl.when(jnp.logical_or(i == 0, be_ref[i] != be_ref[jnp.maximum(i - 1, 0)]))
        def _():
            w1b[...] = w1_ref[...].astype(BF16)
            w3b[...] = w3_ref[...].astype(BF16)
            w2b[...] = w2_ref[...].astype(BF16)

        pltpu.make_async_copy(h_hbm.at[pl.ds(0, rows * chunks), :], xbuf.at[buf], sem.at[buf]).wait()
        x = _load_token_major(xbuf.at[buf], rows, chunks).astype(BF16)
        a = jnp.dot(x, w1b[...], preferred_element_type=F32)
        b = jnp.dot(x, w3b[...], preferred_element_type=F32)
        y = jnp.dot((_silu(a) * b).astype(BF16), w2b[...], preferred_element_type=F32)
        _store_token_major(y_ref, y)

    @pl.when(i >= n_used)
    def _():
        y_ref[...] = jnp.zeros(y_ref.shape, F32)


def _moe_dispatch(ids, rows):
    n_tok, k = ids.shape
    nk = n_tok * k
    flat_e = ids.reshape(nk)
    onehot = (flat_e[:, None] == jnp.arange(N_EXPERTS, dtype=I32)[None, :]).astype(I32)
    csum = jnp.cumsum(onehot, axis=0)
    counts = csum[-1]
    pos = jnp.sum((csum - onehot) * onehot, axis=1)
    padded = (counts + rows - 1) // rows * rows
    pad_end = jnp.cumsum(padded)
    pad_start = jnp.sum(jnp.where(onehot > 0, (pad_end - padded)[None, :], 0), axis=1)
    dest = (pad_start + pos).astype(I32)
    n_blocks = (nk + rows - 1) // rows + N_EXPERTS
    starts = jnp.arange(n_blocks, dtype=I32) * rows
    block_expert = jnp.minimum(jnp.sum((pad_end[None, :] <= starts[:, None]).astype(I32), axis=1),
                               N_EXPERTS - 1).astype(I32)
    n_used = (pad_end[-1] // rows).astype(I32).reshape(1)
    return dest, block_expert, n_used


def _moe_experts(h_all, dest, block_expert, n_used, w1, w3, w2, layer, top_k):
    rows = MOE_ROWS
    n_blocks = block_expert.shape[0]
    d, d_e = w1.shape[-2:]
    chunks = d // LANES
    grid_spec = pltpu.PrefetchScalarGridSpec(
        num_scalar_prefetch=3,
        grid=(n_blocks,),
        in_specs=[pl.BlockSpec(memory_space=pl.ANY),
                  pl.BlockSpec((None, None, d, d_e), lambda i, be, dst, nu: (layer, be[i], 0, 0)),
                  pl.BlockSpec((None, None, d, d_e), lambda i, be, dst, nu: (layer, be[i], 0, 0)),
                  pl.BlockSpec((None, None, d_e, d), lambda i, be, dst, nu: (layer, be[i], 0, 0))],
        out_specs=pl.BlockSpec((rows * chunks, LANES), lambda i, be, dst, nu: (i, 0)),
        scratch_shapes=[pltpu.SMEM((n_blocks * rows,), I32), pltpu.VMEM((MOE_ROW_BUFFERS, rows * chunks, LANES), F32),
                        pltpu.VMEM((d, d_e), BF16), pltpu.VMEM((d, d_e), BF16), pltpu.VMEM((d_e, d), BF16),
                        pltpu.SemaphoreType.DMA((MOE_ROW_BUFFERS,))],
    )
    return pl.pallas_call(
        functools.partial(_moe_kernel, rows=rows, chunks=chunks, top_k=top_k),
        grid_spec=grid_spec,
        out_shape=jax.ShapeDtypeStruct((n_blocks * rows * chunks, LANES), F32),
        compiler_params=_cparams(1),
        name="moe_experts",
    )(block_expert, dest, n_used, h_all, w1, w3, w2)


def _combine_kernel(dest_ref, x_ref, y_hbm, wts_ref, gate_ref, lng_ref, lnb_ref, *refs, tm, chunks, top_k, alpha,
                    has_next):
    if has_next:
        shift_ref, scale_ref, x2_ref, h_ref, ybuf, sem = refs
    else:
        x2_ref, ybuf, sem = refs
    i = pl.program_id(0)

    def gather(tile, slot):
        base = tile * (top_k * tm)

        def issue(r, c):
            for k in range(top_k):
                _token_copy(y_hbm, dest_ref[base + top_k * r + k], ybuf.at[slot, k], r, chunks, sem.at[slot]).start()
            return c

        lax.fori_loop(0, tm, issue, 0, unroll=4)

    @pl.when(i == 0)
    def _():
        gather(0, 0)

    slot = lax.rem(i, 2)

    @pl.when(i + 1 < pl.num_programs(0))
    def _():
        gather(i + 1, 1 - slot)

    for k in range(top_k):
        pltpu.make_async_copy(y_hbm.at[pl.ds(0, tm * chunks), :], ybuf.at[slot, k], sem.at[slot]).wait()
    wts = wts_ref[...]
    g = sum(wts[:, k:k + 1] * _load_token_major(ybuf.at[slot, k], tm, chunks) for k in range(top_k))
    x2 = _layer_norm(alpha * x_ref[...] + (1.0 + gate_ref[...]) * g, lng_ref[...], lnb_ref[...])
    x2_ref[...] = x2
    if has_next:
        h_ref[...] = (x2 * (1.0 + scale_ref[...]) + shift_ref[...]).astype(h_ref.dtype)


def _combine(x1, y_slots, dest, wts, mod, layer, ln_g, ln_b, *, tm, alpha, has_next, top_k):
    m, d = x1.shape
    chunks = d // LANES
    row = lambda i, dst: (i, 0)
    vec = pl.BlockSpec((1, d), lambda i, dst: (0, 0))
    in_specs = [pl.BlockSpec((tm, d), row), pl.BlockSpec(memory_space=pl.ANY), pl.BlockSpec((tm, LANES), row),
                mod.spec(layer, 5), vec, vec]
    args = [x1, y_slots, wts, mod.array, ln_g.reshape(1, d), ln_b.reshape(1, d)]
    out_specs = [pl.BlockSpec((tm, d), row)]
    out_shape = [jax.ShapeDtypeStruct((m, d), F32)]
    if has_next:
        in_specs += [mod.spec(layer + 1, 0), mod.spec(layer + 1, 1)]
        args += [mod.array, mod.array]
        out_specs.append(pl.BlockSpec((tm, d), row))
        out_shape.append(jax.ShapeDtypeStruct((m, d), BF16))
    grid_spec = pltpu.PrefetchScalarGridSpec(
        num_scalar_prefetch=1, grid=(m // tm,), in_specs=in_specs, out_specs=out_specs,
        scratch_shapes=[pltpu.VMEM((2, top_k, tm * chunks, LANES), F32), pltpu.SemaphoreType.DMA((2,))])
    out = pl.pallas_call(
        functools.partial(_combine_kernel, tm=tm, chunks=chunks, top_k=top_k, alpha=alpha, has_next=has_next),
        grid_spec=grid_spec, out_shape=out_shape, compiler_params=_cparams(1), name="moe_combine",
    )(dest, *args)
    return (out[0], out[1]) if has_next else (out[0], None)


def kernel(x_prompt, x_sample, cache_k, cache_v, state_conv, page_table, c_prompt, c_sample, rel_bias_table,
           att_w_qkv, att_w_o, lambda_q1, lambda_k1, lambda_q2, lambda_k2, subln_g, conv_w_in, conv_w, conv_w_out,
           ada_w, ada_b, ln_g, ln_b, moe_w_group, moe_b_group, moe_w_expert, moe_b_expert, moe_w1, moe_w3, moe_w2):
    n_batch, seq, d = x_prompt.shape
    n_dec, t_new, _ = x_sample.shape
    depth = ada_w.shape[0]
    n_att = att_w_qkv.shape[0]
    d_att = N_HEADS * HEAD_DIM
    alpha = (2.0 * depth) ** 0.25
    mp, ms = n_batch * seq, n_dec * t_new
    n_all = mp + ms
    tm_p, tm_s = ROW_TILE, ms
    assert mp % ms == 0 and seq % tm_p == 0 and cache_k.shape[3] == N_HEADS and cache_k.shape[4] == HEAD_DIM

    n_seq = n_batch + n_dec
    pad = -n_seq % 8
    c_all = jnp.concatenate([c_prompt, c_sample, jnp.zeros((pad, d), F32)], axis=0)
    mod_all = _adaln(c_all, ada_w, ada_b).reshape(depth, n_seq + pad, 6, d)
    mod_p = _Mod(mod_all[:, :n_batch], seq, tm_p, per_row=False)
    mod_s_rows = jnp.repeat(jnp.swapaxes(mod_all[:, n_batch:n_seq], 1, 2), t_new, axis=2)
    mod_s = _Mod(mod_s_rows.reshape(depth * 6, ms, d), t_new, tm_s, per_row=True)

    xp = x_prompt.reshape(mp, d)
    xs = x_sample.reshape(ms, d)
    hp = _modulate(xp, mod_p, 0, tm_p)
    hs = _modulate(xs, mod_s, 0, tm_s)

    tiles = _prompt_bias_tiles(rel_bias_table, ATT_TILE)
    zero_state = jnp.zeros((n_batch, CONV_WIDTH - 1, d), F32)
    kp_buf = vp_buf = True
    ks_rows, vs_rows, conv_p, conv_s = [], [], [], []

    for l in range(depth):
        if l % 2 == 0:
            a = l // 2
            lam_init = 0.8 - 0.6 * math.exp(-0.3 * l)
            lam = (jnp.exp(jnp.sum(lambda_q1[a] * lambda_k1[a])) - jnp.exp(jnp.sum(lambda_q2[a] * lambda_k2[a]))
                   + lam_init).astype(F32)
            w_qkv = att_w_qkv[a].astype(BF16)
            wq, wk, wv = w_qkv[:, :d_att], w_qkv[:, d_att:2 * d_att], w_qkv[:, 2 * d_att:]
            (q_p,) = _project(hp, wq, tm=1024, tn=1024, out_dtypes=[BF16])
            kp_buf, k_p = _project(hp, wk, tm=1024, tn=1024, out_dtypes=[F32, BF16], stacked=kp_buf, layer=a,
                                   n_stack=n_att)
            vp_buf, v_p = _project(hp, wv, tm=1024, tn=1024, out_dtypes=[F32, BF16], stacked=vp_buf, layer=a,
                                   n_stack=n_att)
            o_p = _prompt_attention(q_p, k_p, v_p, kp_buf, a, tiles, lam, subln_g[a], lam_init, n_batch, seq)
            (qkv_s,) = _project(hs, w_qkv, tm=ms, tn=1024, out_dtypes=[F32])
            q_s, k_s, v_s = qkv_s[:, :d_att], qkv_s[:, d_att:2 * d_att], qkv_s[:, 2 * d_att:]
            ks_rows.append(k_s.reshape(n_dec, t_new, N_HEADS, HEAD_DIM))
            vs_rows.append(v_s.reshape(n_dec, t_new, N_HEADS, HEAD_DIM))
            o_s = _sample_attention(q_s, k_s, v_s, cache_k, cache_v, page_table, a, rel_bias_table, lam,
                                    subln_g[a], lam_init)
            w_out = att_w_o[a].astype(BF16)
        else:
            ci = l // 2
            w_in = conv_w_in[ci].astype(BF16)
            o_p, st_p = _conv_in(hp, w_in, conv_w[ci], zero_state, rows_per_seq=seq, tm=512, tn=512)
            o_s, st_s = _conv_in_short(hs, w_in, conv_w[ci], state_conv[ci], t=t_new, tn=512)
            conv_p.append(st_p)
            conv_s.append(st_s)
            w_out = conv_w_out[ci].astype(BF16)

        w_router = jnp.zeros((d, LANES), F32).at[:, :N_GROUPS].set(moe_w_group[l]) \
            .at[:, N_GROUPS:N_GROUPS + N_EXPERTS].set(moe_w_expert[l]).astype(BF16)
        b_router = jnp.zeros((1, LANES), F32).at[0, :N_GROUPS].set(moe_b_group[l]) \
            .at[0, N_GROUPS:N_GROUPS + N_EXPERTS].set(moe_b_expert[l])
        xp, h2_all, ids_p, wts_p = _post_mixer(o_p, w_out, xp, mod_p, l, ln_g[l, 0], ln_b[l, 0], w_router, b_router,
                                               tm=tm_p, alpha=alpha, h2_all=None, row0=0, n_all=n_all)
        xs, h2_all, ids_s, wts_s = _post_mixer(o_s, w_out, xs, mod_s, l, ln_g[l, 0], ln_b[l, 0], w_router, b_router,
                                               tm=tm_s, alpha=alpha, h2_all=h2_all, row0=mp, n_all=n_all)
        ids = jnp.concatenate([ids_p[:, :TOP_EXPERT], ids_s[:, :TOP_EXPERT]], axis=0)
        dest, block_expert, n_used = _moe_dispatch(ids, MOE_ROWS)
        y_slots = _moe_experts(h2_all, dest, block_expert, n_used, moe_w1, moe_w3, moe_w2, l, TOP_EXPERT)
        has_next = l + 1 < depth
        xp, hp = _combine(xp, y_slots, dest[:TOP_EXPERT * mp], wts_p, mod_p, l, ln_g[l, 1], ln_b[l, 1], tm=tm_p,
                          alpha=alpha, has_next=has_next, top_k=TOP_EXPERT)
        xs, hs = _combine(xs, y_slots, dest[TOP_EXPERT * mp:], wts_s, mod_s, l, ln_g[l, 1], ln_b[l, 1], tm=tm_s,
                          alpha=alpha, has_next=has_next, top_k=TOP_EXPERT)

    kv_shape = (n_att, n_batch, seq, N_HEADS, HEAD_DIM)
    return (xp.reshape(n_batch, seq, d), xs.reshape(n_dec, t_new, d),
            kp_buf.reshape(kv_shape), vp_buf.reshape(kv_shape), jnp.stack(conv_p),
            jnp.stack(ks_rows), jnp.stack(vs_rows), jnp.stack(conv_s))
```

```python
import functools
import math

import jax
import jax.numpy as jnp
from jax import lax
from jax.experimental import pallas as pl
from jax.experimental.pallas import tpu as pltpu

F32, BF16, I32 = jnp.float32, jnp.bfloat16, jnp.int32
HIGHEST = lax.Precision.HIGHEST

HEAD_DIM = 128
HALF_DIM = HEAD_DIM // 2
N_HEADS_A = 8
N_HEADS_B = 8
N_HEADS = N_HEADS_A + N_HEADS_B
MOBA_BLOCK = 256
MOBA_TOPK = 3
REL_BUCKETS = 32
REL_MAX_DIST = 128
N_GROUPS = 4
EXPERTS_PER_GROUP = 8
N_EXPERTS = N_GROUPS * EXPERTS_PER_GROUP
TOP_EXPERT = 2
CONV_WIDTH = 3
LN_EPS = 1e-5
NEG_INF = -1e30

LANES = 128
ATT_TILE = MOBA_BLOCK
ROW_TILE = 256
MOE_ROWS = 256
MOE_ROW_BUFFERS = 3
PAGES_PER_STEP = 8
VMEM_LIMIT = 56 << 20


def _cparams(n_axes, vmem=VMEM_LIMIT):
    return pltpu.CompilerParams(dimension_semantics=("arbitrary",) * n_axes, vmem_limit_bytes=vmem)


def _smem_spec():
    return pl.BlockSpec(memory_space=pltpu.SMEM)


def _layer_norm(y, g, b):
    mu = jnp.mean(y, axis=-1, keepdims=True)
    yc = y - mu
    var = jnp.mean(yc * yc, axis=-1, keepdims=True)
    return yc * lax.rsqrt(var + LN_EPS) * g + b


def _silu(x):
    return x * (1.0 / (1.0 + jnp.exp(-x)))


def _dot_nt(a, b):
    return lax.dot_general(a, b, (((1,), (1,)), ((), ())), preferred_element_type=F32)


def _adaln_kernel(c_ref, w_ref, b_ref, o_ref):
    s = _silu(c_ref[...]).astype(BF16)
    o_ref[...] = jnp.dot(s, w_ref[...].astype(BF16), preferred_element_type=F32) + b_ref[...]


def _adaln(c_all, ada_w, ada_b):
    depth, d, n = ada_w.shape
    rows = c_all.shape[0]
    tn = 1024
    return pl.pallas_call(
        _adaln_kernel,
        grid=(depth, n // tn),
        in_specs=[pl.BlockSpec((rows, d), lambda l, j: (0, 0)),
                  pl.BlockSpec((None, d, tn), lambda l, j: (l, 0, j)),
                  pl.BlockSpec((None, 1, tn), lambda l, j: (l, 0, j))],
        out_specs=pl.BlockSpec((None, rows, tn), lambda l, j: (l, 0, j)),
        out_shape=jax.ShapeDtypeStruct((depth, rows, n), F32),
        compiler_params=_cparams(2),
        name="adaln",
    )(c_all, ada_w, ada_b.reshape(depth, 1, n))


class _Mod:
    def __init__(self, table, rows_per_seq, row_tile, per_row):
        self.table, self.rows_per_seq, self.row_tile, self.per_row = table, rows_per_seq, row_tile, per_row

    def spec(self, layer, which, tn=None, row_arg=0, col_arg=None):
        d = self.table.shape[-1]
        tn = d if tn is None else tn
        tiles_per_seq = max(self.rows_per_seq // self.row_tile, 1)
        n_seq = self.table.shape[1] if not self.per_row else None

        def index(*g):
            col = 0 if col_arg is None else g[col_arg]
            if self.per_row:
                return (layer * 6 + which, g[row_arg], col)
            return ((layer * n_seq + g[row_arg] // tiles_per_seq) * 6 + which, 0, col)

        rows = self.row_tile if self.per_row else 1
        return pl.BlockSpec((None, rows, tn), index)

    @property
    def array(self):
        if self.per_row:
            return self.table
        depth, n_seq, six, d = self.table.shape
        return self.table.reshape(depth * n_seq * six, 1, d)


def _vec_spec(d):
    return pl.BlockSpec((1, d), lambda *g: (0, 0))


def _modulate_kernel(x_ref, shift_ref, scale_ref, h_ref):
    h_ref[...] = (x_ref[...] * (1.0 + scale_ref[...]) + shift_ref[...]).astype(h_ref.dtype)


def _modulate(x, mod, layer, tm):
    m, d = x.shape
    return pl.pallas_call(
        _modulate_kernel,
        grid=(m // tm,),
        in_specs=[pl.BlockSpec((tm, d), lambda i: (i, 0)), mod.spec(layer, 0), mod.spec(layer, 1)],
        out_specs=pl.BlockSpec((tm, d), lambda i: (i, 0)),
        out_shape=jax.ShapeDtypeStruct((m, d), BF16),
        compiler_params=_cparams(1),
        name="modulate",
    )(x, mod.array, mod.array)


def _matmul_kernel(x_ref, w_ref, *refs, n_alias):
    acc = jnp.dot(x_ref[...], w_ref[...], preferred_element_type=F32)
    for o_ref in refs[n_alias:]:
        o_ref[...] = acc.astype(o_ref.dtype)


def _project(x, w, *, tm, tn, out_dtypes, stacked=None, layer=0, n_stack=1, col0=0, n=None):
    m, k = x.shape
    n = w.shape[1] if n is None else n
    assert col0 % tn == 0 and n % tn == 0
    in_specs = [pl.BlockSpec((tm, k), lambda i, j: (i, 0)),
                pl.BlockSpec((k, tn), lambda i, j: (0, col0 // tn + j))]
    args = [x, w]
    out_specs, out_shape, aliases = [], [], {}
    n_alias = 0
    for dt in out_dtypes:
        if dt == F32 and stacked is not None:
            out_specs.append(pl.BlockSpec((None, tm, tn), lambda i, j: (layer, i, j)))
            out_shape.append(jax.ShapeDtypeStruct((n_stack, m, n), F32))
            if stacked is not True:
                in_specs.append(pl.BlockSpec(memory_space=pl.ANY))
                args.append(stacked)
                aliases[len(args) - 1] = len(out_shape) - 1
                n_alias = 1
        else:
            out_specs.append(pl.BlockSpec((tm, tn), lambda i, j: (i, j)))
            out_shape.append(jax.ShapeDtypeStruct((m, n), dt))
    return pl.pallas_call(
        functools.partial(_matmul_kernel, n_alias=n_alias),
        grid=(m // tm, n // tn),
        in_specs=in_specs, out_specs=out_specs, out_shape=out_shape,
        input_output_aliases=aliases,
        compiler_params=_cparams(2),
        name="project",
    )(*args)


def _t5_bucket(n):
    n = jnp.maximum(n, 0)
    max_exact = REL_BUCKETS // 2
    nf = jnp.maximum(n, 1).astype(F32)
    large = max_exact + (jnp.log(nf / max_exact) / math.log(REL_MAX_DIST / max_exact)
                         * (REL_BUCKETS - max_exact)).astype(I32)
    return jnp.where(n < max_exact, n, jnp.minimum(large, REL_BUCKETS - 1))


def _bias_of_distance(table, dist):
    bucket = _t5_bucket(dist)
    out = jnp.zeros((table.shape[1],) + bucket.shape, F32)
    for b in range(REL_BUCKETS):
        out = out + jnp.where(bucket == b, table[b].reshape((-1,) + (1,) * bucket.ndim), 0.0)
    return out


def _prompt_bias_tiles(table, t):
    assert t >= REL_MAX_DIST
    kk = jnp.arange(t)[:, None]
    qq = jnp.arange(t)[None, :]
    d0 = qq - kk
    far = table[REL_BUCKETS - 1][:, None, None]
    same = jnp.where(d0 >= 0, (_bias_of_distance(table, d0) - far) * LOG2E, NEG_INF)
    prev = (_bias_of_distance(table, d0 + t) - far) * LOG2E
    return jnp.stack([same, prev], axis=1).astype(F32)


LOG2E = 1.4426950408889634
V_ROWS = HEAD_DIM + 16


def _transpose_v(v_ref, vt_ref, t):
    ones = jnp.ones((V_ROWS - HEAD_DIM, t), BF16)
    for c in range(v_ref.shape[0] // t):
        vt_ref[c, 0:HEAD_DIM, :] = v_ref[c * t:(c + 1) * t, :].astype(F32).T.astype(BF16)
        vt_ref[c, HEAD_DIM:V_ROWS, :] = ones


def _attend(ts, vts, m_ref, acc_ref, idx, masks=None):
    m_old = m_ref[idx]
    masks = [None] * len(ts) if masks is None else masks
    tops = [jnp.max(tt, axis=0, keepdims=True) for tt in ts]
    tops = [tp if mk is None else jnp.where(mk > 0.0, tp, NEG_INF) for tp, mk in zip(tops, masks)]
    m_new = functools.reduce(jnp.maximum, tops, m_old)
    acc = jnp.exp2(m_old - m_new) * acc_ref[idx]
    for tt, vt, mk in zip(ts, vts, masks):
        m_use = m_new if mk is None else jnp.where(mk > 0.0, m_new, -NEG_INF)
        acc = acc + jnp.dot(vt, jnp.exp2(tt - m_use).astype(BF16), preferred_element_type=F32)
    acc_ref[idx] = acc
    m_ref[idx] = m_new


SWEEP = 4


def _sweep_key_blocks(qi, step, near_biases):
    n_past = jnp.maximum(qi - 1, 0)
    done = (n_past // SWEEP) * SWEEP
    rest = n_past - done

    @pl.when(qi == 0)
    def _():
        step([qi], near_biases[:1])

    for r in range(SWEEP):
        @pl.when(jnp.logical_and(qi >= 1, rest == r))
        def _():
            step([qi, qi - 1] + [done + u for u in range(r)], near_biases + [None] * r)

    def sweep(jj, c):
        step([SWEEP * jj + u for u in range(SWEEP)], [None] * SWEEP)
        return c

    lax.fori_loop(0, n_past // SWEEP, sweep, 0)


def _init_stats(m_ref, l_ref, acc_ref):
    m_ref[...] = jnp.full(m_ref.shape, NEG_INF, F32)
    l_ref[...] = jnp.zeros(l_ref.shape, F32)
    acc_ref[...] = jnp.zeros(acc_ref.shape, F32)


def _key_block(k_ref, j, t):
    return k_ref[pl.ds(pl.multiple_of(j * t, t), t), :]


def _diff_attn_kernel(lam_ref, q_ref, k_ref, v_ref, bias_ref, g_ref, o_ref, vt_ref, m_ref, acc_ref, *, t, out_scale):
    qi = pl.program_id(2)

    @pl.when(qi == 0)
    def _():
        _transpose_v(v_ref, vt_ref, t)

    q = q_ref[...]
    lane = lax.broadcasted_iota(I32, q.shape, 1)
    qs = (jnp.where(lane < HALF_DIM, q, jnp.zeros_like(q)), jnp.where(lane >= HALF_DIM, q, jnp.zeros_like(q)))
    scale = (HALF_DIM ** -0.5) * LOG2E
    m_ref[...] = jnp.full(m_ref.shape, NEG_INF, F32)
    acc_ref[...] = jnp.zeros(acc_ref.shape, F32)

    def step(blocks, biases):
        for idx in range(2):
            ts = [_dot_nt(_key_block(k_ref, j, t), qs[idx]) * scale for j in blocks]
            ts = [tt if bias is None else tt + bias for tt, bias in zip(ts, biases)]
            _attend(ts, [vt_ref[j] for j in blocks], m_ref, acc_ref, idx)

    _sweep_key_blocks(qi, step, [bias_ref[0], bias_ref[1]])

    o = (acc_ref[0, 0:HEAD_DIM] / acc_ref[0, HEAD_DIM:HEAD_DIM + 1]
         - lam_ref[0] * (acc_ref[1, 0:HEAD_DIM] / acc_ref[1, HEAD_DIM:HEAD_DIM + 1]))
    ms = jnp.mean(o * o, axis=0, keepdims=True)
    o = o * lax.rsqrt(ms + LN_EPS) * g_ref[...] * out_scale
    o_ref[...] = o.T.astype(o_ref.dtype)


def _moba_attn_kernel(q_ref, k_ref, v_ref, kf_ref, bias_ref, o_in_ref, o_ref,
                      vt_ref, kmean_ref, sel_ref, m_ref, acc_ref, *, t):
    del o_in_ref
    qi = pl.program_id(2)
    nb = k_ref.shape[0] // t

    @pl.when(qi == 0)
    def _():
        _transpose_v(v_ref, vt_ref, t)
        for c in range(nb):
            kmean_ref[c:c + 1, :] = jnp.mean(kf_ref[c * t:(c + 1) * t, :], axis=0, keepdims=True)

    q = q_ref[...]
    gate = _dot_nt(kmean_ref[...].astype(BF16), q)
    blk = lax.broadcasted_iota(I32, gate.shape, 0)
    rank = jnp.zeros(gate.shape, F32)
    for n in range(nb):
        row = gate[n:n + 1, :]
        beats = jnp.logical_or(row > gate, jnp.logical_and(row == gate, n < blk))
        rank = rank + jnp.where(jnp.logical_and(beats, n < qi), 1.0, 0.0)
    sel_ref[...] = jnp.where(jnp.logical_and(blk < qi, rank < MOBA_TOPK), 1.0, 0.0)

    scale = (HEAD_DIM ** -0.5) * LOG2E
    m_ref[...] = jnp.full(m_ref.shape, NEG_INF, F32)
    acc_ref[...] = jnp.zeros(acc_ref.shape, F32)

    def step(blocks, biases):
        ts = [_dot_nt(_key_block(k_ref, j, t), q) * scale for j in blocks]
        ts = [tt if bias is None else tt + bias for tt, bias in zip(ts, biases)]
        masks = [None if j is qi else sel_ref[pl.ds(j, 1), :] for j in blocks]
        _attend(ts, [vt_ref[j] for j in blocks], m_ref, acc_ref, 0, masks)

    _sweep_key_blocks(qi, step, [bias_ref[0], bias_ref[1]])
    o_ref[...] = (acc_ref[0, 0:HEAD_DIM] / acc_ref[0, HEAD_DIM:HEAD_DIM + 1]).T.astype(o_ref.dtype)


def _prompt_attention(q, k, v, k_f32, layer, tiles, lam, subln_g, lam_init, n_batch, seq):
    m, d_att = q.shape
    t = ATT_TILE
    assert seq % t == 0 and t == MOBA_BLOCK
    nq = seq // t
    nb = seq // t
    na = N_HEADS_A

    def specs(h0):
        return [pl.BlockSpec((t, HEAD_DIM), lambda b, h, i: (b * nq + i, h0 + h)),
                pl.BlockSpec((seq, HEAD_DIM), lambda b, h, i: (b, h0 + h)),
                pl.BlockSpec((seq, HEAD_DIM), lambda b, h, i: (b, h0 + h))]

    def tile_spec(h0):
        return pl.BlockSpec((None, 2, t, t), lambda b, h, i: (h0 + h, 0, 0, 0))

    stats = [pltpu.VMEM((2, 1, t), F32), pltpu.VMEM((2, V_ROWS, t), F32)]
    o = pl.pallas_call(
        functools.partial(_diff_attn_kernel, t=t, out_scale=1.0 - lam_init),
        grid=(n_batch, na, nq),
        in_specs=[_smem_spec()] + specs(0) + [tile_spec(0), pl.BlockSpec((HEAD_DIM, 1), lambda b, h, i: (0, 0))],
        out_specs=pl.BlockSpec((t, HEAD_DIM), lambda b, h, i: (b * nq + i, h)),
        out_shape=jax.ShapeDtypeStruct((m, d_att), BF16),
        scratch_shapes=[pltpu.VMEM((nb, V_ROWS, t), BF16)] + stats,
        compiler_params=_cparams(3),
        name="diff_attention",
    )(lam.reshape(1), q, k, v, tiles, subln_g.reshape(HEAD_DIM, 1))
    return pl.pallas_call(
        functools.partial(_moba_attn_kernel, t=t),
        grid=(n_batch, N_HEADS_B, nq),
        in_specs=specs(na) + [pl.BlockSpec((None, seq, HEAD_DIM), lambda b, h, i: (layer, b, na + h)),
                              tile_spec(na), pl.BlockSpec(memory_space=pl.ANY)],
        out_specs=pl.BlockSpec((t, HEAD_DIM), lambda b, h, i: (b * nq + i, na + h)),
        out_shape=jax.ShapeDtypeStruct((m, d_att), BF16),
        scratch_shapes=[pltpu.VMEM((nb, V_ROWS, t), BF16), pltpu.VMEM((nb, HEAD_DIM), F32),
                        pltpu.VMEM((nb, t), F32)] + stats,
        input_output_aliases={5: 0},
        compiler_params=_cparams(3),
        name="moba_attention",
    )(q, k, v, k_f32, tiles, o)


def _decode_kernel(pt_ref, lam_ref, q_ref, knew_ref, vnew_ref, far_ref, dlast_ref, bnew_ref, g_ref, *refs,
                   n_pages_step, n_steps, n_blocks, out_scale):
    del pt_ref
    p_step = n_pages_step
    k_pages = refs[:p_step]
    v_pages = refs[p_step:2 * p_step]
    o_ref = refs[2 * p_step]
    (qrows_ref, s_ref, p_ref, pv_ref, kpad_ref, vpad_ref, m_ref, l_ref, acc_ref,
     gate_ref, pm_ref, pl_ref, po_ref) = refs[2 * p_step + 1:]
    step = pl.program_id(1)
    t_new = q_ref.shape[0]
    hr = 2 * t_new
    half = N_HEADS_A * hr
    page = LANES
    pages_per_blk = MOBA_BLOCK // page
    blks_per_step = p_step // pages_per_blk
    na = N_HEADS_A

    def rows(h):
        return slice(h * hr, (h + 1) * hr)

    def cols(h):
        return slice(h * HEAD_DIM, (h + 1) * HEAD_DIM)

    def page_head(ref, h):
        return ref[pl.ds(h, page, stride=N_HEADS), :].astype(BF16)

    @pl.when(step == 0)
    def _():
        for h in range(N_HEADS):
            qh = q_ref[:, cols(h)]
            if h < na:
                qh = qh * (HALF_DIM ** -0.5)
                lane = lax.broadcasted_iota(I32, qh.shape, 1)
                blk = jnp.concatenate([jnp.where(lane < HALF_DIM, qh, 0.0), jnp.where(lane >= HALF_DIM, qh, 0.0)],
                                      axis=0)
            else:
                blk = jnp.concatenate([qh * (HEAD_DIM ** -0.5), jnp.zeros_like(qh)], axis=0)
            qrows_ref[rows(h), :] = blk.astype(BF16)
        _init_stats(m_ref, l_ref, acc_ref)
        gate_ref[...] = jnp.zeros(gate_ref.shape, F32)
        pm_ref[...] = jnp.zeros(pm_ref.shape, F32)
        pl_ref[...] = jnp.zeros(pl_ref.shape, F32)

    def diff_softmax(s, width):
        m_old = m_ref[...]
        m_new = jnp.maximum(m_old, jnp.max(s, axis=1, keepdims=True))
        a = jnp.exp(m_old - m_new)
        p = jnp.exp(s - m_new)
        l_ref[...] = a * l_ref[...] + jnp.sum(p, axis=1, keepdims=True)
        m_ref[...] = m_new
        p_ref[0:half, 0:width] = p.astype(BF16)
        return a

    for h in range(N_HEADS):
        qh = qrows_ref[rows(h), :]
        for u in range(p_step):
            s_ref[rows(h), u * page:(u + 1) * page] = _dot_nt(qh, page_head(k_pages[u], h))

    last_f = jnp.where(step == n_steps - 1, 1.0, 0.0)
    bias = far_ref[...] + dlast_ref[...] * last_f
    a = diff_softmax(s_ref[0:half, :] + bias[0:half], p_step * page)
    raw = s_ref[half:, :]
    sb = raw + bias[half:]
    blk_lane = lax.broadcasted_iota(I32, pm_ref.shape, 1)
    for c in range(blks_per_step):
        kc = slice(c * MOBA_BLOCK, (c + 1) * MOBA_BLOCK)
        hit = blk_lane == step * blks_per_step + c
        m_c = jnp.max(sb[:, kc], axis=1, keepdims=True)
        p_c = jnp.exp(sb[:, kc] - m_c)
        gate_ref[...] = jnp.where(hit, jnp.sum(raw[:, kc], axis=1, keepdims=True), gate_ref[...])
        pm_ref[...] = jnp.where(hit, m_c, pm_ref[...])
        pl_ref[...] = jnp.where(hit, jnp.sum(p_c, axis=1, keepdims=True), pl_ref[...])
        p_ref[half:, kc] = p_c.astype(BF16)

    def pv(h, us):
        ph = p_ref[rows(h), :]
        return sum(jnp.dot(ph[:, u * page:(u + 1) * page], page_head(v_pages[u], h), preferred_element_type=F32)
                   for u in us)

    for h in range(na):
        pv_ref[rows(h), :] = pv(h, range(p_step))
    for hb in range(N_HEADS_B):
        for c in range(blks_per_step):
            po_ref[step * blks_per_step + c, rows(hb), :] = pv(na + hb, range(c * pages_per_blk,
                                                                                (c + 1) * pages_per_blk))
    acc_ref[...] = a * acc_ref[...] + pv_ref[0:half, :]

    @pl.when(step == n_steps - 1)
    def _():
        kpad_ref[...] = jnp.zeros(kpad_ref.shape, F32)
        vpad_ref[...] = jnp.zeros(vpad_ref.shape, F32)
        kpad_ref[0:t_new, :] = knew_ref[...]
        vpad_ref[0:t_new, :] = vnew_ref[...]
        for h in range(N_HEADS):
            s_ref[rows(h), 0:page] = _dot_nt(qrows_ref[rows(h), :], kpad_ref[:, cols(h)].astype(BF16))
        sn = s_ref[:, 0:page] + bnew_ref[...]
        a_new = diff_softmax(sn[0:half], page)
        m_own = jnp.max(sn[half:], axis=1, keepdims=True)
        p_own = jnp.exp(sn[half:] - m_own)
        l_own = jnp.sum(p_own, axis=1, keepdims=True)
        p_ref[half:, 0:page] = p_own.astype(BF16)
        for h in range(N_HEADS):
            pv_ref[rows(h), :] = jnp.dot(p_ref[rows(h), 0:page], vpad_ref[:, cols(h)].astype(BF16),
                                         preferred_element_type=F32)
        o = (a_new * acc_ref[...] + pv_ref[0:half, :]) / l_ref[...]
        lam = lam_ref[0]
        for h in range(na):
            od = o[h * hr:h * hr + t_new] - lam * o[h * hr + t_new:(h + 1) * hr]
            ms = jnp.mean(od * od, axis=1, keepdims=True)
            o_ref[:, cols(h)] = od * lax.rsqrt(ms + LN_EPS) * g_ref[...] * out_scale

        lane_f = blk_lane.astype(F32)
        gate = jnp.where(blk_lane < n_blocks, gate_ref[...], -jnp.inf)
        sel = jnp.zeros(gate.shape, jnp.bool_)
        for _ in range(min(MOBA_TOPK, n_blocks)):
            best = jnp.max(gate, axis=1, keepdims=True)
            first = jnp.min(jnp.where(gate == best, lane_f, float(LANES)), axis=1, keepdims=True)
            pick = lane_f == first
            sel = jnp.logical_or(sel, pick)
            gate = jnp.where(pick, -jnp.inf, gate)
        pm = pm_ref[...]
        m_all = jnp.maximum(m_own, jnp.max(jnp.where(sel, pm, NEG_INF), axis=1, keepdims=True))
        w = jnp.where(sel, jnp.exp(jnp.where(sel, pm, NEG_INF) - m_all), 0.0)
        w_own = jnp.exp(m_own - m_all)
        l_all = w_own * l_own + jnp.sum(w * pl_ref[...], axis=1, keepdims=True)
        o_all = w_own * pv_ref[half:, :]
        for n in range(n_blocks):
            o_all = o_all + w[:, n:n + 1] * po_ref[n]
        o_all = o_all / l_all
        for hb in range(N_HEADS_B):
            o_ref[:, cols(na + hb)] = o_all[hb * hr:hb * hr + t_new]


def _sample_attention(q, k_new, v_new, cache_k, cache_v, page_table, layer, table, lam, subln_g, lam_init):
    n_seq, n_pages = page_table.shape
    t_new = q.shape[0] // n_seq
    d_att = q.shape[1]
    page = cache_k.shape[2]
    past = n_pages * page
    p_step = PAGES_PER_STEP
    assert page == LANES and n_pages % p_step == 0 and past % MOBA_BLOCK == 0 and p_step % (MOBA_BLOCK // page) == 0
    assert page >= REL_MAX_DIST and t_new == 8 and past // MOBA_BLOCK <= LANES
    n_steps = n_pages // p_step
    n_blocks = past // MOBA_BLOCK
    hr = 2 * t_new
    n_rows = N_HEADS * hr
    half = N_HEADS_A * hr
    ck = cache_k.reshape(cache_k.shape[0], cache_k.shape[1], page * N_HEADS, HEAD_DIM)
    cv = cache_v.reshape(cache_v.shape[0], cache_v.shape[1], page * N_HEADS, HEAD_DIM)

    tt = jnp.arange(t_new)[:, None]
    rr = jnp.arange(page)[None, :]
    far = table[REL_BUCKETS - 1]
    dlast = _bias_of_distance(table, page + tt - rr) - far[:, None, None]
    dnew = tt - rr
    bnew = jnp.where(jnp.logical_and(dnew >= 0, rr < t_new), _bias_of_distance(table, dnew), NEG_INF)
    is_diff = (jnp.arange(N_HEADS) < N_HEADS_A)[:, None, None]
    dlast = jnp.concatenate([dlast, jnp.where(is_diff, dlast, 0.0)], axis=1).reshape(n_rows, page)
    bnew = jnp.concatenate([bnew, jnp.where(is_diff, bnew, 0.0)], axis=1).reshape(n_rows, page)
    dlast = jnp.concatenate([jnp.zeros((n_rows, (p_step - 1) * page), F32), dlast.astype(F32)], axis=1)
    far_rows = jnp.repeat(far.astype(F32), hr).reshape(n_rows, 1)

    def page_spec(u):
        return pl.BlockSpec((None, None, page * N_HEADS, HEAD_DIM),
                            lambda b, s, pt: (layer, pt[b * n_pages + s * p_step + u], 0, 0))

    def full(shape):
        return pl.BlockSpec(shape, lambda b, s, pt: (0,) * len(shape))

    row_spec = pl.BlockSpec((t_new, d_att), lambda b, s, pt: (b, 0))
    grid_spec = pltpu.PrefetchScalarGridSpec(
        num_scalar_prefetch=1,
        grid=(n_seq, n_steps),
        in_specs=[_smem_spec(), row_spec, row_spec, row_spec, full((n_rows, 1)), full((n_rows, p_step * page)),
                  full((n_rows, page)), full((1, HEAD_DIM))]
                 + [page_spec(u) for u in range(p_step)] * 2,
        out_specs=row_spec,
        scratch_shapes=[pltpu.VMEM((n_rows, HEAD_DIM), BF16), pltpu.VMEM((n_rows, p_step * page), F32),
                        pltpu.VMEM((n_rows, p_step * page), BF16), pltpu.VMEM((n_rows, HEAD_DIM), F32),
                        pltpu.VMEM((page, d_att), F32), pltpu.VMEM((page, d_att), F32),
                        pltpu.VMEM((half, 1), F32), pltpu.VMEM((half, 1), F32), pltpu.VMEM((half, HEAD_DIM), F32),
                        pltpu.VMEM((n_rows - half, LANES), F32), pltpu.VMEM((n_rows - half, LANES), F32),
                        pltpu.VMEM((n_rows - half, LANES), F32),
                        pltpu.VMEM((n_blocks, n_rows - half, HEAD_DIM), F32)],
    )
    return pl.pallas_call(
        functools.partial(_decode_kernel, n_pages_step=p_step, n_steps=n_steps, n_blocks=n_blocks,
                          out_scale=1.0 - lam_init),
        grid_spec=grid_spec,
        out_shape=jax.ShapeDtypeStruct(q.shape, F32),
        compiler_params=_cparams(2),
        name="decode_attention",
    )(page_table.reshape(-1), lam.reshape(1), q, k_new, v_new, far_rows, dlast, bnew.astype(F32),
      subln_g.reshape(1, HEAD_DIM), *([ck] * p_step), *([cv] * p_step))


def _conv_in_kernel(x_ref, wb_ref, wc_ref, wv_ref, cw_ref, st_ref, z_ref, tail_ref, carry_ref, *, tiles_per_seq):
    i = pl.program_id(1)
    x = x_ref[...]
    gb = jnp.dot(x, wb_ref[...], preferred_element_type=F32)
    u = jnp.dot(x, wc_ref[...], preferred_element_type=F32) * jnp.dot(x, wv_ref[...], preferred_element_type=F32)
    tm = u.shape[0]

    @pl.when(i % tiles_per_seq == 0)
    def _():
        carry_ref[...] = st_ref[...]

    prev = carry_ref[...]
    row = lax.broadcasted_iota(I32, u.shape, 0)
    u1 = jnp.where(row == 0, prev[1:2], pltpu.roll(u, 1, axis=0))
    u2 = jnp.where(row == 0, prev[0:1], jnp.where(row == 1, prev[1:2], pltpu.roll(u, 2, axis=0)))
    cw = cw_ref[...]
    z_ref[...] = (gb * (cw[0:1] * u2 + cw[1:2] * u1 + cw[2:3] * u)).astype(z_ref.dtype)
    tail = u[tm - (CONV_WIDTH - 1):tm]
    carry_ref[...] = tail
    tail_ref[...] = tail


def _conv_in(h, w_in, conv_w, state, *, rows_per_seq, tm, tn):
    m, d = h.shape
    n_seq = m // rows_per_seq
    assert rows_per_seq % tm == 0 and tm >= CONV_WIDTH - 1
    tiles_per_seq = rows_per_seq // tm
    nj = d // tn
    st_spec = pl.BlockSpec((None, CONV_WIDTH - 1, tn), lambda j, i: (i // tiles_per_seq, 0, j))
    return pl.pallas_call(
        functools.partial(_conv_in_kernel, tiles_per_seq=tiles_per_seq),
        grid=(nj, m // tm),
        in_specs=[pl.BlockSpec((tm, d), lambda j, i: (i, 0))] + _w_in_specs(d, tn, nj, lambda j, i: j)
                 + [pl.BlockSpec((CONV_WIDTH, tn), lambda j, i: (0, j)), st_spec],
        out_specs=[pl.BlockSpec((tm, tn), lambda j, i: (i, j)), st_spec],
        out_shape=[jax.ShapeDtypeStruct((m, d), BF16), jax.ShapeDtypeStruct((n_seq, CONV_WIDTH - 1, d), F32)],
        scratch_shapes=[pltpu.VMEM((CONV_WIDTH - 1, tn), F32)],
        compiler_params=_cparams(2),
        name="conv_in",
    )(h, w_in, w_in, w_in, conv_w, state)


def _w_in_specs(d, tn, nj, col_of):
    def spec(part):
        return pl.BlockSpec((d, tn), lambda *g: (0, part * nj + col_of(*g)))
    return [spec(part) for part in range(3)]


def _conv_in_short_kernel(x_ref, wb_ref, wc_ref, wv_ref, cw_ref, p0_ref, p1_ref, z_ref, u_ref, *, t):
    x = x_ref[...]
    gb = jnp.dot(x, wb_ref[...], preferred_element_type=F32)
    u = jnp.dot(x, wc_ref[...], preferred_element_type=F32) * jnp.dot(x, wv_ref[...], preferred_element_type=F32)
    pos = lax.rem(lax.broadcasted_iota(I32, u.shape, 0), t)
    u1 = jnp.where(pos == 0, p1_ref[...], pltpu.roll(u, 1, axis=0))
    u2 = jnp.where(pos == 0, p0_ref[...], jnp.where(pos == 1, p1_ref[...], pltpu.roll(u, 2, axis=0)))
    cw = cw_ref[...]
    z_ref[...] = (gb * (cw[0:1] * u2 + cw[1:2] * u1 + cw[2:3] * u)).astype(z_ref.dtype)
    u_ref[...] = u


def _conv_in_short(h, w_in, conv_w, state, *, t, tn):
    m, d = h.shape
    n_seq = m // t
    assert t >= CONV_WIDTH - 1
    nj = d // tn
    p0 = jnp.repeat(state[:, 0, :], t, axis=0)
    p1 = jnp.repeat(state[:, 1, :], t, axis=0)
    col = pl.BlockSpec((m, tn), lambda j: (0, j))
    z, u = pl.pallas_call(
        functools.partial(_conv_in_short_kernel, t=t),
        grid=(nj,),
        in_specs=[pl.BlockSpec((m, d), lambda j: (0, 0))] + _w_in_specs(d, tn, nj, lambda j: j)
                 + [pl.BlockSpec((CONV_WIDTH, tn), lambda j: (0, j)), col, col],
        out_specs=[col, col],
        out_shape=[jax.ShapeDtypeStruct((m, d), BF16), jax.ShapeDtypeStruct((m, d), F32)],
        compiler_params=_cparams(1),
        name="conv_in_short",
    )(h, w_in, w_in, w_in, conv_w, p0, p1)
    return z, u.reshape(n_seq, t, d)[:, t - (CONV_WIDTH - 1):]


def _route(logits):
    lane = lax.broadcasted_iota(I32, logits.shape, 1).astype(F32)
    g_mask = lane < N_GROUPS
    g_exp = jnp.exp(jnp.where(g_mask, logits, -jnp.inf)
                    - jnp.max(jnp.where(g_mask, logits, -jnp.inf), axis=1, keepdims=True))
    g_prob = g_exp / jnp.sum(g_exp, axis=1, keepdims=True)
    g_p = jnp.max(g_prob, axis=1, keepdims=True)
    g_idx = jnp.min(jnp.where(jnp.logical_and(g_prob == g_p, g_mask), lane, float(LANES)), axis=1, keepdims=True)
    lo = N_GROUPS + EXPERTS_PER_GROUP * g_idx
    e_mask = jnp.logical_and(lane >= lo, lane < lo + EXPERTS_PER_GROUP)
    e_logit = jnp.where(e_mask, logits, -jnp.inf)
    e_exp = jnp.exp(e_logit - jnp.max(e_logit, axis=1, keepdims=True))
    e_prob = jnp.where(e_mask, e_exp / jnp.sum(e_exp, axis=1, keepdims=True), -1.0)
    p1 = jnp.max(e_prob, axis=1, keepdims=True)
    i1 = jnp.min(jnp.where(e_prob == p1, lane, float(LANES)), axis=1, keepdims=True)
    rest = jnp.where(lane == i1, -1.0, e_prob)
    p2 = jnp.max(rest, axis=1, keepdims=True)
    i2 = jnp.min(jnp.where(rest == p2, lane, float(LANES)), axis=1, keepdims=True)
    tot = p1 + p2
    ids = jnp.where(lane == 0.0, i1 - N_GROUPS, jnp.where(lane == 1.0, i2 - N_GROUPS, 0.0)).astype(I32)
    wts = jnp.where(lane == 0.0, g_p * (p1 / tot), jnp.where(lane == 1.0, g_p * (p2 / tot), 0.0))
    return ids, wts


def _store_token_major(ref, x):
    rows, width = x.shape
    s = width // LANES
    for c in range(s):
        ref[pl.ds(c, rows, stride=s), :] = x[:, c * LANES:(c + 1) * LANES]


def _load_token_major(ref, rows, s):
    return jnp.concatenate([ref[pl.ds(c, rows, stride=s), :] for c in range(s)], axis=1)


def _token_copy(src, tok, dst, slot, s, sem):
    return pltpu.make_async_copy(src.at[pl.ds(pl.multiple_of(tok * s, s), s), :],
                                 dst.at[pl.ds(pl.multiple_of(slot * s, s), s), :], sem)


def _post_mixer_kernel(o_ref, w_ref, x_ref, gate_ref, lng_ref, lnb_ref, shift_ref, scale_ref, wr_ref, br_ref,
                       *refs, alpha, n_alias):
    x1_ref, h2_ref, ids_ref, wts_ref = refs[n_alias:]
    f = jnp.dot(o_ref[...].astype(BF16), w_ref[...], preferred_element_type=F32)
    x1 = _layer_norm(alpha * x_ref[...] + (1.0 + gate_ref[...]) * f, lng_ref[...], lnb_ref[...])
    x1_ref[...] = x1
    h2 = x1 * (1.0 + scale_ref[...]) + shift_ref[...]
    _store_token_major(h2_ref, h2)
    logits = jnp.dot(h2.astype(BF16), wr_ref[...], preferred_element_type=F32) + br_ref[...]
    ids_ref[...], wts_ref[...] = _route(logits)


def _post_mixer(o, w, x, mod, layer, ln_g, ln_b, w_router, b_router, *, tm, alpha, h2_all, row0, n_all):
    m, d = x.shape
    k = o.shape[1]
    assert row0 % tm == 0
    blk0 = row0 // tm
    row = lambda i: (i, 0)
    in_specs = [pl.BlockSpec((tm, k), row), pl.BlockSpec((k, d), lambda i: (0, 0)), pl.BlockSpec((tm, d), row),
                mod.spec(layer, 2), _vec_spec(d), _vec_spec(d), mod.spec(layer, 3), mod.spec(layer, 4),
                pl.BlockSpec((d, LANES), lambda i: (0, 0)), _vec_spec(LANES)]
    args = [o, w, x, mod.array, ln_g.reshape(1, d), ln_b.reshape(1, d), mod.array, mod.array, w_router, b_router]
    aliases, n_alias = {}, 0
    if h2_all is not None:
        in_specs.append(pl.BlockSpec(memory_space=pl.ANY))
        args.append(h2_all)
        aliases, n_alias = {len(args) - 1: 1}, 1
    return pl.pallas_call(
        functools.partial(_post_mixer_kernel, alpha=alpha, n_alias=n_alias),
        grid=(m // tm,),
        in_specs=in_specs,
        out_specs=[pl.BlockSpec((tm, d), row), pl.BlockSpec((tm * (d // LANES), LANES), lambda i: (blk0 + i, 0)),
                   pl.BlockSpec((tm, LANES), row), pl.BlockSpec((tm, LANES), row)],
        out_shape=[jax.ShapeDtypeStruct((m, d), F32), jax.ShapeDtypeStruct((n_all * (d // LANES), LANES), F32),
                   jax.ShapeDtypeStruct((m, LANES), I32), jax.ShapeDtypeStruct((m, LANES), F32)],
        input_output_aliases=aliases,
        compiler_params=_cparams(1),
        name="post_mixer",
    )(*args)


def _moe_kernel(be_ref, dest_ref, nused_ref, h_hbm, w1_ref, w3_ref, w2_ref, y_ref,
                slot_ref, xbuf, w1b, w3b, w2b, sem, *, rows, chunks, top_k):
    i = pl.program_id(0)
    n_used = nused_ref[0]
    n_buf = xbuf.shape[0]

    def gather(block, buf):
        base = block * rows

        def issue(r, c):
            _token_copy(h_hbm, lax.div(slot_ref[base + r], top_k), xbuf.at[buf], r, chunks, sem.at[buf]).start()
            return c

        lax.fori_loop(0, rows, issue, 0, unroll=8)

    @pl.when(i == 0)
    def _():
        def clear(s, c):
            slot_ref[s] = 0
            return c

        lax.fori_loop(0, slot_ref.shape[0], clear, 0, unroll=8)

        def place(a, c):
            slot_ref[dest_ref[a]] = a
            return c

        lax.fori_loop(0, dest_ref.shape[0], place, 0, unroll=8)
        for ahead in range(n_buf - 1):
            @pl.when(ahead < n_used)
            def _():
                gather(ahead, ahead)

    @pl.when(i < n_used)
    def _():
        buf = lax.rem(i, n_buf)

        @pl.when(i + (n_buf - 1) < n_used)
        def _():
            gather(i + (n_buf - 1), lax.rem(i + (n_buf - 1), n_buf))

        @pl.when(jnp.logical_or(i == 0, be_ref[i] != be_ref[jnp.maximum(i - 1, 0)]))
        def _():
            w1b[...] = w1_ref[...].astype(BF16)
            w3b[...] = w3_ref[...].astype(BF16)
            w2b[...] = w2_ref[...].astype(BF16)

        pltpu.make_async_copy(h_hbm.at[pl.ds(0, rows * chunks), :], xbuf.at[buf], sem.at[buf]).wait()
        x = _load_token_major(xbuf.at[buf], rows, chunks).astype(BF16)
        a = jnp.dot(x, w1b[...], preferred_element_type=F32)
        b = jnp.dot(x, w3b[...], preferred_element_type=F32)
        y = jnp.dot((_silu(a) * b).astype(BF16), w2b[...], preferred_element_type=F32)
        _store_token_major(y_ref, y)

    @pl.when(i >= n_used)
    def _():
        y_ref[...] = jnp.zeros(y_ref.shape, F32)


def _moe_dispatch(ids, rows):
    n_tok, k = ids.shape
    nk = n_tok * k
    flat_e = ids.reshape(nk)
    onehot = (flat_e[:, None] == jnp.arange(N_EXPERTS, dtype=I32)[None, :]).astype(I32)
    csum = jnp.cumsum(onehot, axis=0)
    counts = csum[-1]
    pos = jnp.sum((csum - onehot) * onehot, axis=1)
    padded = (counts + rows - 1) // rows * rows
    pad_end = jnp.cumsum(padded)
    pad_start = jnp.sum(jnp.where(onehot > 0, (pad_end - padded)[None, :], 0), axis=1)
    dest = (pad_start + pos).astype(I32)
    n_blocks = (nk + rows - 1) // rows + N_EXPERTS
    starts = jnp.arange(n_blocks, dtype=I32) * rows
    block_expert = jnp.minimum(jnp.sum((pad_end[None, :] <= starts[:, None]).astype(I32), axis=1),
                               N_EXPERTS - 1).astype(I32)
    n_used = (pad_end[-1] // rows).astype(I32).reshape(1)
    return dest, block_expert, n_used


def _moe_experts(h_all, dest, block_expert, n_used, w1, w3, w2, layer, top_k):
    rows = MOE_ROWS
    n_blocks = block_expert.shape[0]
    d, d_e = w1.shape[-2:]
    chunks = d // LANES
    grid_spec = pltpu.PrefetchScalarGridSpec(
        num_scalar_prefetch=3,
        grid=(n_blocks,),
        in_specs=[pl.BlockSpec(memory_space=pl.ANY),
                  pl.BlockSpec((None, None, d, d_e), lambda i, be, dst, nu: (layer, be[i], 0, 0)),
                  pl.BlockSpec((None, None, d, d_e), lambda i, be, dst, nu: (layer, be[i], 0, 0)),
                  pl.BlockSpec((None, None, d_e, d), lambda i, be, dst, nu: (layer, be[i], 0, 0))],
        out_specs=pl.BlockSpec((rows * chunks, LANES), lambda i, be, dst, nu: (i, 0)),
        scratch_shapes=[pltpu.SMEM((n_blocks * rows,), I32), pltpu.VMEM((MOE_ROW_BUFFERS, rows * chunks, LANES), F32),
                        pltpu.VMEM((d, d_e), BF16), pltpu.VMEM((d, d_e), BF16), pltpu.VMEM((d_e, d), BF16),
                        pltpu.SemaphoreType.DMA((MOE_ROW_BUFFERS,))],
    )
    return pl.pallas_call(
        functools.partial(_moe_kernel, rows=rows, chunks=chunks, top_k=top_k),
        grid_spec=grid_spec,
        out_shape=jax.ShapeDtypeStruct((n_blocks * rows * chunks, LANES), F32),
        compiler_params=_cparams(1),
        name="moe_experts",
    )(block_expert, dest, n_used, h_all, w1, w3, w2)


def _combine_kernel(dest_ref, x_ref, y_hbm, wts_ref, gate_ref, lng_ref, lnb_ref, *refs, tm, chunks, top_k, alpha,
                    has_next):
    if has_next:
        shift_ref, scale_ref, x2_ref, h_ref, ybuf, sem = refs
    else:
        x2_ref, ybuf, sem = refs
    i = pl.program_id(0)

    def gather(tile, slot):
        base = tile * (top_k * tm)

        def issue(r, c):
            for k in range(top_k):
                _token_copy(y_hbm, dest_ref[base + top_k * r + k], ybuf.at[slot, k], r, chunks, sem.at[slot]).start()
            return c

        lax.fori_loop(0, tm, issue, 0, unroll=4)

    @pl.when(i == 0)
    def _():
        gather(0, 0)

    slot = lax.rem(i, 2)

    @pl.when(i + 1 < pl.num_programs(0))
    def _():
        gather(i + 1, 1 - slot)

    for k in range(top_k):
        pltpu.make_async_copy(y_hbm.at[pl.ds(0, tm * chunks), :], ybuf.at[slot, k], sem.at[slot]).wait()
    wts = wts_ref[...]
    g = sum(wts[:, k:k + 1] * _load_token_major(ybuf.at[slot, k], tm, chunks) for k in range(top_k))
    x2 = _layer_norm(alpha * x_ref[...] + (1.0 + gate_ref[...]) * g, lng_ref[...], lnb_ref[...])
    x2_ref[...] = x2
    if has_next:
        h_ref[...] = (x2 * (1.0 + scale_ref[...]) + shift_ref[...]).astype(h_ref.dtype)


def _combine(x1, y_slots, dest, wts, mod, layer, ln_g, ln_b, *, tm, alpha, has_next, top_k):
    m, d = x1.shape
    chunks = d // LANES
    row = lambda i, dst: (i, 0)
    vec = pl.BlockSpec((1, d), lambda i, dst: (0, 0))
    in_specs = [pl.BlockSpec((tm, d), row), pl.BlockSpec(memory_space=pl.ANY), pl.BlockSpec((tm, LANES), row),
                mod.spec(layer, 5), vec, vec]
    args = [x1, y_slots, wts, mod.array, ln_g.reshape(1, d), ln_b.reshape(1, d)]
    out_specs = [pl.BlockSpec((tm, d), row)]
    out_shape = [jax.ShapeDtypeStruct((m, d), F32)]
    if has_next:
        in_specs += [mod.spec(layer + 1, 0), mod.spec(layer + 1, 1)]
        args += [mod.array, mod.array]
        out_specs.append(pl.BlockSpec((tm, d), row))
        out_shape.append(jax.ShapeDtypeStruct((m, d), BF16))
    grid_spec = pltpu.PrefetchScalarGridSpec(
        num_scalar_prefetch=1, grid=(m // tm,), in_specs=in_specs, out_specs=out_specs,
        scratch_shapes=[pltpu.VMEM((2, top_k, tm * chunks, LANES), F32), pltpu.SemaphoreType.DMA((2,))])
    out = pl.pallas_call(
        functools.partial(_combine_kernel, tm=tm, chunks=chunks, top_k=top_k, alpha=alpha, has_next=has_next),
        grid_spec=grid_spec, out_shape=out_shape, compiler_params=_cparams(1), name="moe_combine",
    )(dest, *args)
    return (out[0], out[1]) if has_next else (out[0], None)


def kernel(x_prompt, x_sample, cache_k, cache_v, state_conv, page_table, c_prompt, c_sample, rel_bias_table,
           att_w_qkv, att_w_o, lambda_q1, lambda_k1, lambda_q2, lambda_k2, subln_g, conv_w_in, conv_w, conv_w_out,
           ada_w, ada_b, ln_g, ln_b, moe_w_group, moe_b_group, moe_w_expert, moe_b_expert, moe_w1, moe_w3, moe_w2):
    n_batch, seq, d = x_prompt.shape
    n_dec, t_new, _ = x_sample.shape
    depth = ada_w.shape[0]
    n_att = att_w_qkv.shape[0]
    d_att = N_HEADS * HEAD_DIM
    alpha = (2.0 * depth) ** 0.25
    mp, ms = n_batch * seq, n_dec * t_new
    n_all = mp + ms
    tm_p, tm_s = ROW_TILE, ms
    assert mp % ms == 0 and seq % tm_p == 0 and cache_k.shape[3] == N_HEADS and cache_k.shape[4] == HEAD_DIM

    n_seq = n_batch + n_dec
    pad = -n_seq % 8
    c_all = jnp.concatenate([c_prompt, c_sample, jnp.zeros((pad, d), F32)], axis=0)
    mod_all = _adaln(c_all, ada_w, ada_b).reshape(depth, n_seq + pad, 6, d)
    mod_p = _Mod(mod_all[:, :n_batch], seq, tm_p, per_row=False)
    mod_s_rows = jnp.repeat(jnp.swapaxes(mod_all[:, n_batch:n_seq], 1, 2), t_new, axis=2)
    mod_s = _Mod(mod_s_rows.reshape(depth * 6, ms, d), t_new, tm_s, per_row=True)

    xp = x_prompt.reshape(mp, d)
    xs = x_sample.reshape(ms, d)
    hp = _modulate(xp, mod_p, 0, tm_p)
    hs = _modulate(xs, mod_s, 0, tm_s)

    tiles = _prompt_bias_tiles(rel_bias_table, ATT_TILE)
    zero_state = jnp.zeros((n_batch, CONV_WIDTH - 1, d), F32)
    kp_buf = vp_buf = True
    ks_rows, vs_rows, conv_p, conv_s = [], [], [], []

    for l in range(depth):
        if l % 2 == 0:
            a = l // 2
            lam_init = 0.8 - 0.6 * math.exp(-0.3 * l)
            lam = (jnp.exp(jnp.sum(lambda_q1[a] * lambda_k1[a])) - jnp.exp(jnp.sum(lambda_q2[a] * lambda_k2[a]))
                   + lam_init).astype(F32)
            w_qkv = att_w_qkv[a].astype(BF16)
            (q_p,) = _project(hp, w_qkv, tm=1024, tn=1024, out_dtypes=[BF16], col0=0, n=d_att)
            kp_buf, k_p = _project(hp, w_qkv, tm=1024, tn=1024, out_dtypes=[F32, BF16], stacked=kp_buf, layer=a,
                                   n_stack=n_att, col0=d_att, n=d_att)
            vp_buf, v_p = _project(hp, w_qkv, tm=1024, tn=1024, out_dtypes=[F32, BF16], stacked=vp_buf, layer=a,
                                   n_stack=n_att, col0=2 * d_att, n=d_att)
            o_p = _prompt_attention(q_p, k_p, v_p, kp_buf, a, tiles, lam, subln_g[a], lam_init, n_batch, seq)
            (qkv_s,) = _project(hs, w_qkv, tm=ms, tn=1024, out_dtypes=[F32])
            q_s, k_s, v_s = qkv_s[:, :d_att], qkv_s[:, d_att:2 * d_att], qkv_s[:, 2 * d_att:]
            ks_rows.append(k_s.reshape(n_dec, t_new, N_HEADS, HEAD_DIM))
            vs_rows.append(v_s.reshape(n_dec, t_new, N_HEADS, HEAD_DIM))
            o_s = _sample_attention(q_s, k_s, v_s, cache_k, cache_v, page_table, a, rel_bias_table, lam,
                                    subln_g[a], lam_init)
            w_out = att_w_o[a].astype(BF16)
        else:
            ci = l // 2
            w_in = conv_w_in[ci].astype(BF16)
            o_p, st_p = _conv_in(hp, w_in, conv_w[ci], zero_state, rows_per_seq=seq, tm=512, tn=min(1024, d))
            o_s, st_s = _conv_in_short(hs, w_in, conv_w[ci], state_conv[ci], t=t_new, tn=512)
            conv_p.append(st_p)
            conv_s.append(st_s)
            w_out = conv_w_out[ci].astype(BF16)

        w_router = jnp.zeros((d, LANES), F32).at[:, :N_GROUPS].set(moe_w_group[l]) \
            .at[:, N_GROUPS:N_GROUPS + N_EXPERTS].set(moe_w_expert[l]).astype(BF16)
        b_router = jnp.zeros((1, LANES), F32).at[0, :N_GROUPS].set(moe_b_group[l]) \
            .at[0, N_GROUPS:N_GROUPS + N_EXPERTS].set(moe_b_expert[l])
        xp, h2_all, ids_p, wts_p = _post_mixer(o_p, w_out, xp, mod_p, l, ln_g[l, 0], ln_b[l, 0], w_router, b_router,
                                               tm=tm_p, alpha=alpha, h2_all=None, row0=0, n_all=n_all)
        xs, h2_all, ids_s, wts_s = _post_mixer(o_s, w_out, xs, mod_s, l, ln_g[l, 0], ln_b[l, 0], w_router, b_router,
                                               tm=tm_s, alpha=alpha, h2_all=h2_all, row0=mp, n_all=n_all)
        ids = jnp.concatenate([ids_p[:, :TOP_EXPERT], ids_s[:, :TOP_EXPERT]], axis=0)
        dest, block_expert, n_used = _moe_dispatch(ids, MOE_ROWS)
        y_slots = _moe_experts(h2_all, dest, block_expert, n_used, moe_w1, moe_w3, moe_w2, l, TOP_EXPERT)
        has_next = l + 1 < depth
        xp, hp = _combine(xp, y_slots, dest[:TOP_EXPERT * mp], wts_p, mod_p, l, ln_g[l, 1], ln_b[l, 1], tm=tm_p,
                          alpha=alpha, has_next=has_next, top_k=TOP_EXPERT)
        xs, hs = _combine(xs, y_slots, dest[TOP_EXPERT * mp:], wts_s, mod_s, l, ln_g[l, 1], ln_b[l, 1], tm=tm_s,
                          alpha=alpha, has_next=has_next, top_k=TOP_EXPERT)

    kv_shape = (n_att, n_batch, seq, N_HEADS, HEAD_DIM)
    return (xp.reshape(n_batch, seq, d), xs.reshape(n_dec, t_new, d),
            kp_buf.reshape(kv_shape), vp_buf.reshape(kv_shape), jnp.stack(conv_p),
            jnp.stack(ks_rows), jnp.stack(vs_rows), jnp.stack(conv_s))
```
